```python
import jax, jax.numpy as jnp
from jax import lax
import numpy as np

D_MODEL = 1024
BATCH = 16
SEQ = 2048
DEPTH = 4

N_EVEN = (DEPTH + 1) // 2
N_ODD = DEPTH // 2
EPS = 1e-6

RET_HEADS = 4
RET_DK = 64
RET_DV = 128
RET_CHUNK = 128
ROPE_BASE = 10000.0
SB_HEADS = 8
SB_DH = 64
SB_BLOCK = 128
HG_HEADS = 4
HG_DK = 128
HG_DV = 128
HG_CHUNK = 16
LRU_WIDTH = 512
LRU_BLOCKS = 8
LRU_BW = LRU_WIDTH // LRU_BLOCKS
CONV_W = 4
LRU_C = 8.0

RET_QK = RET_HEADS * RET_DK
RET_V = RET_HEADS * RET_DV
SB_W = SB_HEADS * SB_DH
EVEN_SIZES = (RET_QK, RET_QK, RET_V, RET_V, SB_W, SB_W, SB_W, SB_W)
EVEN_IN = sum(EVEN_SIZES)
EVEN_MIX = RET_V + SB_W
HG_KW = HG_HEADS * HG_DK
HG_VW = HG_HEADS * HG_DV
ODD_SIZES = (HG_KW, HG_KW, HG_VW, HG_VW, LRU_WIDTH, LRU_WIDTH)
ODD_IN = sum(ODD_SIZES)
ODD_MIX = HG_VW + LRU_WIDTH

kernel_name = 'hybrid_retention_stickbreak_hgrn2_rglru'


def split_cols(a, sizes):
    idx = [int(v) for v in np.cumsum(sizes)[:-1]]
    return jnp.split(a, idx, axis=-1)


def rms_norm(x, w=None):
    xf = x.astype(jnp.float32)
    y = xf * lax.rsqrt(jnp.mean(xf * xf, axis=-1, keepdims=True) + EPS)
    if w is not None:
        y = y * w.astype(jnp.float32)
    return y


def rotary(x):
    S, D = x.shape[1], x.shape[-1]
    half = D // 2
    inv = ROPE_BASE ** (-jnp.arange(half, dtype=jnp.float32) / half)
    ang = jnp.arange(S, dtype=jnp.float32)[:, None] * inv[None, :]
    cos = jnp.cos(ang)[None, :, None, :]
    sin = jnp.sin(ang)[None, :, None, :]
    x1, x2 = x[..., :half], x[..., half:]
    return jnp.concatenate([x1 * cos - x2 * sin, x1 * sin + x2 * cos], axis=-1)


def retention(q, k, v):
    B, S, H, dk = q.shape
    dv = v.shape[-1]
    C = RET_CHUNK
    N = S // C
    f32 = jnp.float32
    log_g = jnp.log(1.0 - 2.0 ** (-5.0 - jnp.arange(H, dtype=f32)))
    q = rotary(q.astype(f32))
    k = rotary(k.astype(f32)) * (dk ** -0.5)
    v = v.astype(f32)
    to_chunks = lambda a: a.reshape(B, N, C, H, a.shape[-1]).transpose(0, 3, 1, 2, 4)
    qc, kc, vc = to_chunks(q), to_chunks(k), to_chunks(v)
    pos = jnp.arange(C, dtype=f32)
    dist = pos[:, None] - pos[None, :]
    decay = jnp.where(dist >= 0, jnp.exp(log_g[:, None, None] * jnp.maximum(dist, 0.0)), 0.0)
    scores = jnp.einsum('bhnid,bhnjd->bhnij', qc, kc) * decay[None, :, None]
    o_intra = jnp.einsum('bhnij,bhnje->bhnie', scores, vc)
    k_decay = jnp.exp(log_g[:, None] * (C - 1.0 - pos)[None, :])
    q_decay = jnp.exp(log_g[:, None] * (pos + 1.0)[None, :])
    chunk_kv = jnp.einsum('bhnjd,hj,bhnje->nbhde', kc, k_decay, vc)
    g_chunk = jnp.exp(log_g * C)[None, :, None, None]

    def step(state, kv):
        return state * g_chunk + kv, state

    _, r_prev = lax.scan(step, jnp.zeros((B, H, dk, dv), f32), chunk_kv)
    o_inter = jnp.einsum('bhnid,hi,nbhde->bhnie', qc, q_decay, r_prev)
    return (o_intra + o_inter).transpose(0, 2, 3, 1, 4).reshape(B, S, H, dv)


def stick_breaking(q, k, v):
    B, S, H, d = q.shape
    f32 = jnp.float32
    qf, kf, vf = q.astype(f32), k.astype(f32), v.astype(f32)
    scale = d ** -0.5
    outs = []
    for blk in range(S // SB_BLOCK):
        q0 = blk * SB_BLOCK
        kend = q0 + SB_BLOCK
        z = jnp.einsum('bthd,bshd->bhts', qf[:, q0:kend], kf[:, :kend]) * scale
        t_idx = q0 + jnp.arange(SB_BLOCK)[:, None]
        s_idx = jnp.arange(kend)[None, :]
        strict = s_idx < t_idx
        log_beta = jax.nn.log_sigmoid(z)
        log_one_minus = jnp.where(strict, jax.nn.log_sigmoid(-z), 0.0)
        suffix = lax.cumsum(log_one_minus, axis=3, reverse=True) - log_one_minus
        w = jnp.where(strict, jnp.exp(log_beta + suffix), 0.0)
        outs.append(jnp.einsum('bhts,bshd->bthd', w, vf[:, :kend]))
    return jnp.concatenate(outs, axis=1)


def hgrn2(q, f_logit, i, lb):
    B, S, H, dk = q.shape
    dv = i.shape[-1]
    C = HG_CHUNK
    N = S // C
    f32 = jnp.float32
    lb = lb.reshape(H, dk).astype(f32)
    f_logit = f_logit.astype(f32)
    log_f = jnp.logaddexp(jnp.log(lb), jnp.log1p(-lb) + jax.nn.log_sigmoid(f_logit))
    k = (1.0 - lb) * jax.nn.sigmoid(-f_logit)
    to_chunks = lambda a: a.reshape(B, N, C, H, a.shape[-1]).transpose(1, 0, 3, 2, 4)
    xs = (to_chunks(q.astype(f32)), to_chunks(k), to_chunks(log_f), to_chunks(i.astype(f32)))
    tril = jnp.tril(jnp.ones((C, C), dtype=bool))[:, :, None]

    def step(state, inp):
        qc, kc, lfc, vc = inp
        b = jnp.cumsum(lfc, axis=2)
        diff = b[:, :, :, None, :] - b[:, :, None, :, :]
        decay = jnp.exp(jnp.where(tril, diff, -jnp.inf))
        attn = jnp.einsum('bhtk,bhtsk,bhsk->bhts', qc, decay, kc)
        o = jnp.einsum('bhts,bhsv->bhtv', attn, vc) + jnp.einsum('bhtk,bhkv->bhtv', qc * jnp.exp(b), state)
        b_last = b[:, :, -1:, :]
        state = state * jnp.exp(b_last[:, :, 0, :, None]) + jnp.einsum('bhsk,bhsv->bhkv', kc * jnp.exp(b_last - b), vc)
        return state, o

    _, o = lax.scan(step, jnp.zeros((B, H, dk, dv), f32), xs)
    return o.transpose(1, 0, 3, 2, 4).reshape(B, S, H, dv)


def causal_conv(x, w, b):
    S = x.shape[1]
    xp = jnp.pad(x, ((0, 0), (CONV_W - 1, 0), (0, 0)))
    y = b[None, None, :] + xp[:, 0:S] * w[0]
    for j in range(1, CONV_W):
        y = y + xp[:, j:j + S] * w[j]
    return y


def rg_lru(x, w_a, b_a, w_x, b_x, lam):
    B, S, W = x.shape
    xb = x.reshape(B, S, LRU_BLOCKS, LRU_BW)
    r = jax.nn.sigmoid(jnp.einsum('bsni,nij->bsnj', xb, w_a).reshape(B, S, W) + b_a)
    ig = jax.nn.sigmoid(jnp.einsum('bsni,nij->bsnj', xb, w_x).reshape(B, S, W) + b_x)
    log_a = -LRU_C * r * jax.nn.softplus(-lam)
    a = jnp.exp(log_a)
    mult = jnp.sqrt(-jnp.expm1(2.0 * log_a))
    mult = jnp.where(jnp.arange(S)[None, :, None] == 0, 1.0, mult)
    u = mult * ig * x

    def combine(left, right):
        a1, b1 = left
        a2, b2 = right
        return a1 * a2, a2 * b1 + b2

    _, h = lax.associative_scan(combine, (a, u), axis=1)
    return h


def even_mixer(h, w_in, w_out):
    B, S, _ = h.shape
    proj = jnp.einsum('bsd,de->bse', h.astype(w_in.dtype), w_in).astype(jnp.float32)
    rq, rk, rv, rg, sq, sk, sv, sg = split_cols(proj, EVEN_SIZES)
    ret = retention(rq.reshape(B, S, RET_HEADS, RET_DK), rk.reshape(B, S, RET_HEADS, RET_DK),
                    rv.reshape(B, S, RET_HEADS, RET_DV))
    ret = rms_norm(ret).reshape(B, S, RET_V) * jax.nn.silu(rg)
    sb = stick_breaking(sq.reshape(B, S, SB_HEADS, SB_DH), sk.reshape(B, S, SB_HEADS, SB_DH),
                        sv.reshape(B, S, SB_HEADS, SB_DH))
    sb = sb.reshape(B, S, SB_W) * jax.nn.silu(sg)
    mix = jnp.concatenate([ret, sb], axis=-1)
    return jnp.einsum('bse,ed->bsd', mix.astype(w_out.dtype), w_out)


def odd_mixer(h, w_in, w_out, lb, conv_w, conv_b, w_a, b_a, w_x, b_x, lam):
    B, S, _ = h.shape
    f32 = jnp.float32
    proj = jnp.einsum('bsd,de->bse', h.astype(w_in.dtype), w_in).astype(f32)
    hq, hf, hi, hg, lx, lg = split_cols(proj, ODD_SIZES)
    hc = hgrn2(hq.reshape(B, S, HG_HEADS, HG_DK), hf.reshape(B, S, HG_HEADS, HG_DK),
               hi.reshape(B, S, HG_HEADS, HG_DV), lb)
    hc = rms_norm(hc).reshape(B, S, HG_VW) * jax.nn.silu(hg)
    xc = causal_conv(lx, conv_w.astype(f32), conv_b.astype(f32))
    hl = rg_lru(xc, w_a.astype(f32), b_a.astype(f32), w_x.astype(f32), b_x.astype(f32), lam.astype(f32))
    hl = hl * jax.nn.silu(lg)
    mix = jnp.concatenate([hc, hl], axis=-1)
    return jnp.einsum('bse,ed->bsd', mix.astype(w_out.dtype), w_out)


def setup_inputs(seed: int = 0) -> dict:
    key = jax.random.key(seed)
    ks = jax.random.split(key, 16)
    f32 = jnp.float32
    nrm = lambda k, shape, s: jax.random.normal(k, shape, f32) * s
    u = jax.random.uniform(ks[15], (N_ODD, LRU_WIDTH), f32, 0.9, 0.999)
    a0 = u ** (1.0 / LRU_C)
    return {
        'x': nrm(ks[0], (BATCH, SEQ, D_MODEL), 1.0),
        'pre_norm_w': 1.0 + nrm(ks[1], (DEPTH, D_MODEL), 0.02),
        'post_norm_w': 1.0 + nrm(ks[2], (DEPTH, D_MODEL), 0.02),
        'even_w_in': nrm(ks[3], (N_EVEN, D_MODEL, EVEN_IN), D_MODEL ** -0.5),
        'even_w_out': nrm(ks[4], (N_EVEN, EVEN_MIX, D_MODEL), EVEN_MIX ** -0.5),
        'odd_w_in': nrm(ks[5], (N_ODD, D_MODEL, ODD_IN), D_MODEL ** -0.5),
        'odd_w_out': nrm(ks[6], (N_ODD, ODD_MIX, D_MODEL), ODD_MIX ** -0.5),
        'hgrn_lb_logits': nrm(ks[7], (N_ODD, HG_KW), 1.0),
        'conv_w': nrm(ks[8], (N_ODD, CONV_W, LRU_WIDTH), CONV_W ** -0.5),
        'conv_b': nrm(ks[9], (N_ODD, LRU_WIDTH), 0.01),
        'lru_w_a': nrm(ks[10], (N_ODD, LRU_BLOCKS, LRU_BW, LRU_BW), LRU_BW ** -0.5),
        'lru_b_a': nrm(ks[11], (N_ODD, LRU_WIDTH), 0.01),
        'lru_w_x': nrm(ks[12], (N_ODD, LRU_BLOCKS, LRU_BW, LRU_BW), LRU_BW ** -0.5),
        'lru_b_x': nrm(ks[13], (N_ODD, LRU_WIDTH), 0.01),
        'lru_lambda': jnp.log(a0) - jnp.log1p(-a0),
    }


def reference(x, pre_norm_w, post_norm_w, even_w_in, even_w_out, odd_w_in, odd_w_out,
              hgrn_lb_logits, conv_w, conv_b, lru_w_a, lru_b_a, lru_w_x, lru_b_x, lru_lambda):
    cum = jnp.cumsum(jax.nn.softmax(hgrn_lb_logits.astype(jnp.float32), axis=0), axis=0)
    lower_bounds = cum - cum[0:1]
    for layer in range(DEPTH):
        h = rms_norm(x, pre_norm_w[layer])
        if layer % 2 == 0:
            e = layer // 2
            y = even_mixer(h, even_w_in[e], even_w_out[e])
        else:
            o = layer // 2
            y = odd_mixer(h, odd_w_in[o], odd_w_out[o], lower_bounds[o], conv_w[o], conv_b[o],
                          lru_w_a[o], lru_b_a[o], lru_w_x[o], lru_b_x[o], lru_lambda[o])
        x = x + rms_norm(y, post_norm_w[layer]).astype(x.dtype)
    return x
```

```python
import functools

import numpy as np
import jax
import jax.numpy as jnp
from jax import lax
from jax.experimental import pallas as pl
from jax.experimental.pallas import tpu as pltpu

F32 = jnp.float32
BF16 = jnp.bfloat16

EPS = 1e-6
LANES = 128
TOK = 128
ROPE_BASE = 10000.0
RET_HEADS, RET_DK, RET_DV = 4, 64, 128
SB_HEADS, SB_DH = 8, 64
HG_HEADS = 4
LRU_WIDTH, LRU_BLOCKS, CONV_W, LRU_C = 512, 8, 4, 8.0
F32_EXP_UNDERFLOW = -104.0


def _dot(a, b):
    return jnp.dot(a, b, preferred_element_type=F32)


def _dot_nt(a, b):
    return lax.dot_general(a, b, (((1,), (1,)), ((), ())), preferred_element_type=F32)


def _dot_tn(a, b):
    return lax.dot_general(a, b, (((0,), (0,)), ((), ())), preferred_element_type=F32)


def _log_sigmoid(z):
    return jnp.minimum(z, 0.0) - jnp.log(1.0 + jnp.exp(-jnp.abs(z)))


def _silu(g):
    return g * jax.nn.sigmoid(g)


def _split_bf16(x):
    hi = x.astype(BF16)
    lo = (x - hi.astype(F32)).astype(BF16)
    return hi, lo


def _in_proj_kernel(x_ref, nw_ref, w_ref, o_ref, *, n_chunk):
    x = x_ref[...]
    ms = jnp.mean(x * x, axis=-1, keepdims=True)
    h = (x * lax.rsqrt(ms + EPS) * nw_ref[...]).astype(BF16)
    for c in range(o_ref.shape[1] // n_chunk):
        cols = slice(c * n_chunk, (c + 1) * n_chunk)
        o_ref[:, cols] = _dot(h, w_ref[:, cols]).astype(o_ref.dtype)


def _in_proj(x2d, norm_w, w_bf16, *, tm=512, n_chunk=512):
    m, d = x2d.shape
    n = w_bf16.shape[1]
    return pl.pallas_call(
        functools.partial(_in_proj_kernel, n_chunk=n_chunk),
        grid=(m // tm,),
        in_specs=[
            pl.BlockSpec((tm, d), lambda i: (i, 0)),
            pl.BlockSpec((1, d), lambda i: (0, 0)),
            pl.BlockSpec((d, n), lambda i: (0, 0)),
        ],
        out_specs=pl.BlockSpec((tm, n), lambda i: (i, 0)),
        out_shape=jax.ShapeDtypeStruct((m, n), BF16),
        compiler_params=pltpu.CompilerParams(dimension_semantics=("arbitrary",)),
        name="in_proj",
    )(x2d, norm_w.reshape(1, d), w_bf16)


def _out_proj_kernel(ma_ref, mb_ref, w_ref, nw_ref, x_ref, o_ref):
    half = ma_ref.shape[1]
    y = _dot(ma_ref[...], w_ref[:half, :]) + _dot(mb_ref[...], w_ref[half:, :])
    ms = jnp.mean(y * y, axis=-1, keepdims=True)
    o_ref[...] = x_ref[...] + y * lax.rsqrt(ms + EPS) * nw_ref[...]


def _out_proj(mix_a, mix_b, w_bf16, norm_w, x2d, *, tm=512):
    m, d = x2d.shape
    half = mix_a.shape[1]
    return pl.pallas_call(
        _out_proj_kernel,
        grid=(m // tm,),
        in_specs=[
            pl.BlockSpec((tm, half), lambda i: (i, 0)),
            pl.BlockSpec((tm, half), lambda i: (i, 0)),
            pl.BlockSpec((2 * half, d), lambda i: (0, 0)),
            pl.BlockSpec((1, d), lambda i: (0, 0)),
            pl.BlockSpec((tm, d), lambda i: (i, 0)),
        ],
        out_specs=pl.BlockSpec((tm, d), lambda i: (i, 0)),
        out_shape=jax.ShapeDtypeStruct((m, d), F32),
        compiler_params=pltpu.CompilerParams(dimension_semantics=("arbitrary",)),
        name="out_proj",
    )(mix_a, mix_b, w_bf16, norm_w.reshape(1, d), x2d)


def _retention_kernel(q_ref, q2_ref, k_ref, k2_ref, v_ref, g_ref, cos_ref, sin_ref,
                      decay_ref, qd_ref, kd_ref, o_ref, state_ref, *, bb, g_chunk):
    @pl.when(pl.program_id(1) == 0)
    def _():
        state_ref[...] = jnp.zeros_like(state_ref)

    cos = cos_ref[...]
    sin = sin_ref[...]
    lane = lax.broadcasted_iota(jnp.int32, (TOK, LANES), 1)
    head_mask = (lane < RET_DK, lane >= RET_DK)
    for b in range(bb):
        qr = q_ref[b].astype(F32) * cos + q2_ref[b].astype(F32) * sin
        kr = (k_ref[b].astype(F32) * cos + k2_ref[b].astype(F32) * sin) * (RET_DK ** -0.5)
        for p in range(RET_HEADS // 2):
            lanes = slice(p * LANES, (p + 1) * LANES)
            qp = qr[:, lanes]
            kp = kr[:, lanes].astype(BF16)
            for hh in range(2):
                h = 2 * p + hh
                hl = slice(h * LANES, (h + 1) * LANES)
                qm = jnp.where(head_mask[hh], qp, 0.0)
                v = v_ref[b, :, hl]
                scores = _dot_nt(qm.astype(BF16), kp) * decay_ref[h]
                state = state_ref[b, h]
                o = _dot(scores.astype(BF16), v)
                o = o + _dot((qm * qd_ref[h]).astype(BF16), state.astype(BF16))
                vk = (v.astype(F32) * kd_ref[h]).astype(BF16)
                state_ref[b, h] = state * g_chunk[h] + _dot_tn(kp, vk)
                ms = jnp.mean(o * o, axis=-1, keepdims=True)
                gate = _silu(g_ref[b, :, hl].astype(F32))
                o_ref[b, :, hl] = (o * lax.rsqrt(ms + EPS) * gate).astype(o_ref.dtype)


def _retention_consts(seq):
    h = np.arange(RET_HEADS, dtype=np.float64)
    log_g = np.log(1.0 - 2.0 ** (-5.0 - h))
    pos = np.arange(TOK, dtype=np.float64)
    dist = pos[:, None] - pos[None, :]
    decay = np.where(dist >= 0, np.exp(log_g[:, None, None] * np.maximum(dist, 0.0)), 0.0)
    q_decay = np.exp(log_g[:, None] * (pos + 1.0)[None, :])
    k_decay = np.exp(log_g[:, None] * (TOK - 1.0 - pos)[None, :])
    ones = np.ones((1, 1, LANES))
    g_chunk = tuple(float(v) for v in np.exp(log_g * TOK))
    half = RET_DK // 2
    inv = ROPE_BASE ** (-np.arange(half, dtype=np.float64) / half)
    ang = np.arange(seq, dtype=np.float64)[:, None] * inv[None, :]
    cos = np.tile(np.cos(ang), (1, 2 * RET_HEADS))
    sin = np.tile(np.sin(ang), (1, 2 * RET_HEADS))
    to = lambda a: jnp.asarray(a, dtype=F32)
    return (to(cos), to(sin), to(decay), to(q_decay[:, :, None] * ones),
            to(k_decay[:, :, None] * ones), g_chunk)


def _retention(proj, *, bb=2):
    bsz, seq, _ = proj.shape
    cos, sin, decay, qd, kd, g_chunk = _retention_consts(seq)
    qk_w = RET_HEADS * RET_DK
    v_w = RET_HEADS * RET_DV
    qk_spec = lambda j: pl.BlockSpec((bb, TOK, qk_w), lambda b, c: (b, c, j))
    v_spec = lambda j: pl.BlockSpec((bb, TOK, v_w), lambda b, c: (b, c, j))
    tab_spec = pl.BlockSpec((TOK, qk_w), lambda b, c: (c, 0))
    const_spec = pl.BlockSpec((RET_HEADS, TOK, LANES), lambda b, c: (0, 0, 0))
    return pl.pallas_call(
        functools.partial(_retention_kernel, bb=bb, g_chunk=g_chunk),
        grid=(bsz // bb, seq // TOK),
        in_specs=[qk_spec(0), qk_spec(1), qk_spec(2), qk_spec(3), v_spec(2), v_spec(3),
                  tab_spec, tab_spec, const_spec, const_spec, const_spec],
        out_specs=pl.BlockSpec((bb, TOK, v_w), lambda b, c: (b, c, 0)),
        out_shape=jax.ShapeDtypeStruct((bsz, seq, v_w), BF16),
        scratch_shapes=[pltpu.VMEM((bb, RET_HEADS, LANES, RET_DV), F32)],
        compiler_params=pltpu.CompilerParams(dimension_semantics=("arbitrary", "arbitrary")),
        name="retention",
    )(proj, proj, proj, proj, proj, proj, cos, sin, decay, qd, kd)


def _stick_breaking_kernel(q_ref, k_ref, v_ref, g_ref, tri_ref, o_ref, acc_ref, carry_ref):
    i = pl.program_id(1)
    n_pairs = SB_HEADS // 2
    lane = lax.broadcasted_iota(jnp.int32, (TOK, LANES), 1)
    head_mask = (lane < SB_DH, lane >= SB_DH)
    row = lax.broadcasted_iota(jnp.int32, (TOK, TOK), 0)
    col = lax.broadcasted_iota(jnp.int32, (TOK, TOK), 1)
    strict = col < row
    tri = tri_ref[...]
    scale = SB_DH ** -0.5

    acc_ref[...] = jnp.zeros_like(acc_ref)
    carry_ref[...] = jnp.zeros_like(carry_ref)

    def visit(j, diag):
        rows = pl.ds(pl.multiple_of(j * TOK, TOK), TOK)
        for p in range(n_pairs):
            lanes = slice(p * LANES, (p + 1) * LANES)
            qp = q_ref[0, :, lanes]
            kp = k_ref[0, rows, lanes]
            vp = v_ref[0, rows, lanes]
            for hh in range(2):
                h = 2 * p + hh
                qm = jnp.where(head_mask[hh], qp, jnp.zeros_like(qp)) * scale
                z = _dot_nt(qm, kp)
                log_beta = _log_sigmoid(z)
                log_rest = log_beta - z
                if diag:
                    log_rest = jnp.where(strict, log_rest, 0.0)
                hi, lo = _split_bf16(log_rest)
                sums = _dot(hi, tri) + _dot(lo, tri)
                carry = carry_ref[h]
                w = jnp.exp(log_beta + sums[:, :TOK] + carry)
                if diag:
                    w = jnp.where(strict, w, 0.0)
                vm = jnp.where(head_mask[hh], vp, jnp.zeros_like(vp))
                acc_ref[p] += _dot(w.astype(BF16), vm)
                carry_ref[h] = carry + sums[:, TOK:]

    visit(i, True)

    def cond(state):
        d, top = state
        return jnp.logical_and(d <= i, top >= F32_EXP_UNDERFLOW)

    def body(state):
        d, _ = state
        visit(i - d, False)
        return d + 1, jnp.max(carry_ref[...])

    lax.while_loop(cond, body, (jnp.int32(1), jnp.float32(0.0)))

    for p in range(n_pairs):
        lanes = slice(p * LANES, (p + 1) * LANES)
        gate = _silu(g_ref[0, :, lanes].astype(F32))
        o_ref[0, :, lanes] = (acc_ref[p] * gate).astype(o_ref.dtype)


def _stick_breaking(proj):
    bsz, seq, _ = proj.shape
    w = SB_HEADS * SB_DH
    first = proj.shape[2] // w - 4
    j = np.arange(TOK)
    tri = np.concatenate([(j[:, None] > j[None, :]).astype(np.float32),
                          np.ones((TOK, TOK), np.float32)], axis=1)
    blk = lambda c: pl.BlockSpec((1, TOK, w), lambda b, i: (b, i, c))
    full = lambda c: pl.BlockSpec((1, seq, w), lambda b, i: (b, 0, c))
    return pl.pallas_call(
        _stick_breaking_kernel,
        grid=(bsz, seq // TOK),
        in_specs=[blk(first), full(first + 1), full(first + 2), blk(first + 3),
                  pl.BlockSpec((TOK, 2 * TOK), lambda b, i: (0, 0))],
        out_specs=pl.BlockSpec((1, TOK, w), lambda b, i: (b, i, 0)),
        out_shape=jax.ShapeDtypeStruct((bsz, seq, w), BF16),
        scratch_shapes=[pltpu.VMEM((SB_HEADS // 2, TOK, LANES), F32),
                        pltpu.VMEM((SB_HEADS, TOK, TOK), F32)],
        compiler_params=pltpu.CompilerParams(dimension_semantics=("arbitrary", "arbitrary")),
        name="stick_breaking",
    )(proj, proj, proj, proj, jnp.asarray(tri, dtype=BF16))


HG_LEVELS = tuple(2 ** e for e in range(int(np.log2(TOK))))


def _hgrn_consts():
    t = np.arange(TOK)
    tt, uu = t[:, None], t[None, :]
    mats = [(uu <= tt), (uu > tt)]
    ups, pair_masks = [], [np.eye(TOK, dtype=bool)]
    for m in HG_LEVELS:
        same_block = (tt // (2 * m)) == (uu // (2 * m))
        up_t = (tt % (2 * m)) >= m
        up_u = (uu % (2 * m)) >= m
        same_half = up_t == up_u
        mats.append(same_block & same_half & np.where(up_t, uu <= tt, uu > tt))
        ups.append(np.broadcast_to(up_t, (TOK, LANES)))
        pair_masks.append(same_block & up_t & ~up_u)
    prefix = np.concatenate(mats, axis=0).astype(np.float32)
    return (jnp.asarray(prefix, dtype=BF16), jnp.asarray(np.stack(ups), dtype=F32),
            jnp.asarray(np.stack(pair_masks), dtype=F32))


def _hgrn_kernel(q_ref, f_ref, i_ref, g_ref, lbl_ref, prefix_ref, up_ref, pm_ref, o_ref,
                 state_ref, *, bb, layer):
    @pl.when(pl.program_id(1) == 0)
    def _():
        state_ref[...] = jnp.zeros_like(state_ref)

    logits = lbl_ref[...]
    e = jnp.exp(logits - jnp.max(logits, axis=0, keepdims=True))
    soft = e / jnp.sum(e, axis=0, keepdims=True)
    lb_all = jnp.zeros_like(soft[0:1])
    for r in range(1, layer + 1):
        lb_all = lb_all + soft[r:r + 1]
    prefix = prefix_ref[...]
    n_lv = len(HG_LEVELS)
    for b in range(bb):
        for h in range(HG_HEADS):
            hl = slice(h * LANES, (h + 1) * LANES)
            lb = lb_all[:, hl]
            q = q_ref[b, :, hl].astype(F32)
            fl = f_ref[b, :, hl].astype(F32)
            v = i_ref[b, :, hl]
            ls = _log_sigmoid(fl)
            x1 = jnp.log(lb)
            x2 = jnp.log(1.0 - lb) + ls
            log_f = jnp.maximum(x1, x2) + jnp.log(1.0 + jnp.exp(-jnp.abs(x1 - x2)))
            kk = (1.0 - lb) * jnp.exp(ls - fl)
            hi, lo = _split_bf16(log_f)
            sums = _dot(prefix, hi) + _dot(prefix, lo)
            cum = sums[0:TOK]
            rest = sums[TOK:2 * TOK]
            attn = _dot_nt(q.astype(BF16), kk.astype(BF16)) * pm_ref[0]
            for lv in range(n_lv):
                dec = jnp.exp(sums[(2 + lv) * TOK:(3 + lv) * TOK])
                dec_up = dec * up_ref[lv]
                qe = (q * dec_up).astype(BF16)
                ke = (kk * (dec - dec_up)).astype(BF16)
                attn = attn + _dot_nt(qe, ke) * pm_ref[lv + 1]
            state = state_ref[b, h]
            o = _dot(attn.astype(BF16), v)
            o = o + _dot_nt((q * jnp.exp(cum)).astype(BF16), state.astype(BF16))
            ktail = (kk * jnp.exp(rest)).astype(BF16)
            state_ref[b, h] = state * jnp.exp(cum[TOK - 1:TOK, :]) + _dot_tn(v, ktail)
            ms = jnp.mean(o * o, axis=-1, keepdims=True)
            gate = _silu(g_ref[b, :, hl].astype(F32))
            o_ref[b, :, hl] = (o * lax.rsqrt(ms + EPS) * gate).astype(o_ref.dtype)


def _hgrn(proj, lb_logits, layer, *, bb=2):
    bsz, seq, _ = proj.shape
    w = HG_HEADS * LANES
    prefix, ups, pair_masks = _hgrn_consts()
    blk = lambda c: pl.BlockSpec((bb, TOK, w), lambda b, i: (b, i, c))
    whole = lambda a: pl.BlockSpec(a.shape, lambda b, i: (0,) * a.ndim)
    return pl.pallas_call(
        functools.partial(_hgrn_kernel, bb=bb, layer=layer),
        grid=(bsz // bb, seq // TOK),
        in_specs=[blk(0), blk(1), blk(2), blk(3), whole(lb_logits), whole(prefix), whole(ups),
                  whole(pair_masks)],
        out_specs=pl.BlockSpec((bb, TOK, w), lambda b, i: (b, i, 0)),
        out_shape=jax.ShapeDtypeStruct((bsz, seq, w), BF16),
        scratch_shapes=[pltpu.VMEM((bb, HG_HEADS, LANES, LANES), F32)],
        compiler_params=pltpu.CompilerParams(dimension_semantics=("arbitrary", "arbitrary")),
        name="hgrn2",
    )(proj, proj, proj, proj, lb_logits, prefix, ups, pair_masks)


LRU_TOK = 256
HALO = 8


def _lru_kernel(x_ref, g_ref, cw_ref, cb_ref, wa_ref, ba_ref, wx_ref, bx_ref, lam_ref, o_ref,
                xbuf_ref, h_ref, *, bb):
    c = pl.program_id(1)
    tok = x_ref.shape[1]

    @pl.when(c == 0)
    def _():
        xbuf_ref[...] = jnp.zeros_like(xbuf_ref)
        h_ref[...] = jnp.zeros_like(h_ref)

    lam = lam_ref[...]
    neg_sp = -(jnp.maximum(-lam, 0.0) + jnp.log(1.0 + jnp.exp(-jnp.abs(lam))))
    row = lax.broadcasted_iota(jnp.int32, (tok, LRU_WIDTH), 0)
    first_token = jnp.logical_and(row == 0, c == 0)
    groups = LRU_WIDTH // LANES
    for b in range(bb):
        x = x_ref[b].astype(F32)
        xbuf_ref[b, HALO:HALO + tok, :] = x
        y = cb_ref[...] + x * cw_ref[CONV_W - 1:CONV_W, :]
        for j in range(CONV_W - 1):
            shift = CONV_W - 1 - j
            y = y + xbuf_ref[b, HALO - shift:HALO - shift + tok, :] * cw_ref[j:j + 1, :]
        xbuf_ref[b, 0:HALO, :] = x[tok - HALO:tok, :]
        ra, rx = [], []
        for gidx in range(groups):
            yg = y[:, gidx * LANES:(gidx + 1) * LANES].astype(BF16)
            ra.append(_dot(yg, wa_ref[gidx]))
            rx.append(_dot(yg, wx_ref[gidx]))
        r = jax.nn.sigmoid(jnp.concatenate(ra, axis=1) + ba_ref[...])
        ig = jax.nn.sigmoid(jnp.concatenate(rx, axis=1) + bx_ref[...])
        log_a = LRU_C * r * neg_sp
        a = jnp.exp(log_a)
        mult = jnp.sqrt(-jnp.tanh(log_a) * (1.0 + a * a))
        mult = jnp.where(first_token, 1.0, mult)
        u = mult * ig * y
        d = 1
        while d < tok:
            keep = row >= d
            a_prev = jnp.where(keep, pltpu.roll(a, d, 0), 1.0)
            u_prev = jnp.where(keep, pltpu.roll(u, d, 0), 0.0)
            u = a * u_prev + u
            a = a * a_prev
            d *= 2
        hcur = u + a * h_ref[b]
        h_ref[b] = hcur[tok - 1:tok, :]
        o_ref[b] = (hcur * _silu(g_ref[b].astype(F32))).astype(o_ref.dtype)


def _block_diag_pairs(w):
    n, bw, _ = w.shape
    z = jnp.zeros((n // 2, bw, bw), w.dtype)
    top = jnp.concatenate([w[0::2], z], axis=2)
    bot = jnp.concatenate([z, w[1::2]], axis=2)
    return jnp.concatenate([top, bot], axis=1).astype(BF16)


def _lru(proj, conv_w, conv_b, w_a, b_a, w_x, b_x, lam, *, bb=2):
    bsz, seq, _ = proj.shape
    w = LRU_WIDTH
    first = proj.shape[2] // w - 2
    row = lambda a: a.reshape(1, w).astype(F32)
    blk = lambda c: pl.BlockSpec((bb, LRU_TOK, w), lambda b, i: (b, i, c))
    whole = lambda shape: pl.BlockSpec(shape, lambda b, i: (0,) * len(shape))
    groups = w // LANES
    return pl.pallas_call(
        functools.partial(_lru_kernel, bb=bb),
        grid=(bsz // bb, seq // LRU_TOK),
        in_specs=[blk(first), blk(first + 1), whole((CONV_W, w)), whole((1, w)),
                  whole((groups, LANES, LANES)), whole((1, w)),
                  whole((groups, LANES, LANES)), whole((1, w)), whole((1, w))],
        out_specs=pl.BlockSpec((bb, LRU_TOK, w), lambda b, i: (b, i, 0)),
        out_shape=jax.ShapeDtypeStruct((bsz, seq, w), BF16),
        scratch_shapes=[pltpu.VMEM((bb, HALO + LRU_TOK, w), F32), pltpu.VMEM((bb, 1, w), F32)],
        compiler_params=pltpu.CompilerParams(dimension_semantics=("arbitrary", "arbitrary")),
        name="rg_lru",
    )(proj, proj, conv_w.astype(F32), row(conv_b), _block_diag_pairs(w_a), row(b_a),
      _block_diag_pairs(w_x), row(b_x), row(lam))


def _rotate_half_columns(w, heads, dk):
    d = w.shape[0]
    wh = w.reshape(d, heads, dk)
    half = dk // 2
    return jnp.concatenate([-wh[..., half:], wh[..., :half]], axis=-1).reshape(d, heads * dk)


def _even_in_weights(w_in):
    qk = RET_HEADS * RET_DK
    wq, wk, rest = w_in[:, :qk], w_in[:, qk:2 * qk], w_in[:, 2 * qk:]
    cols = [wq, _rotate_half_columns(wq, RET_HEADS, RET_DK),
            wk, _rotate_half_columns(wk, RET_HEADS, RET_DK), rest]
    return jnp.concatenate(cols, axis=1).astype(BF16)


def kernel(x, pre_norm_w, post_norm_w, even_w_in, even_w_out, odd_w_in, odd_w_out, hgrn_lb_logits,
           conv_w, conv_b, lru_w_a, lru_b_a, lru_w_x, lru_b_x, lru_lambda):
    bsz, seq, d = x.shape
    depth = pre_norm_w.shape[0]
    x2d = x.reshape(bsz * seq, d)
    for layer in range(depth):
        idx = layer // 2
        if layer % 2 == 0:
            proj = _in_proj(x2d, pre_norm_w[layer], _even_in_weights(even_w_in[idx]))
            proj = proj.reshape(bsz, seq, -1)
            mix_a = _retention(proj)
            mix_b = _stick_breaking(proj)
            w_out = even_w_out[idx]
        else:
            proj = _in_proj(x2d, pre_norm_w[layer], odd_w_in[idx].astype(BF16))
            proj = proj.reshape(bsz, seq, -1)
            mix_a = _hgrn(proj, hgrn_lb_logits.astype(F32), idx)
            mix_b = _lru(proj, conv_w[idx], conv_b[idx], lru_w_a[idx], lru_b_a[idx],
                         lru_w_x[idx], lru_b_x[idx], lru_lambda[idx])
            w_out = odd_w_out[idx]
        x2d = _out_proj(mix_a.reshape(bsz * seq, -1), mix_b.reshape(bsz * seq, -1),
                        w_out.astype(BF16), post_norm_w[layer], x2d)
    return x2d.reshape(bsz, seq, d)
```

```python
import functools

import numpy as np
import jax
import jax.numpy as jnp
from jax import lax
from jax.experimental import pallas as pl
from jax.experimental.pallas import tpu as pltpu

F32 = jnp.float32
BF16 = jnp.bfloat16

EPS = 1e-6
LANES = 128
SUBLANES = 8
TOK = 128
ROPE_BASE = 10000.0
RET_HEADS, RET_DK, RET_DV = 4, 64, 128
SB_HEADS, SB_DH = 8, 64
HG_HEADS = 4
LRU_WIDTH, LRU_BLOCKS, CONV_W, LRU_C = 512, 8, 4, 8.0
LOG2E = float(np.log2(np.e))
F32_EXP2_UNDERFLOW = -150.0


def _dot(a, b):
    return jnp.dot(a, b, preferred_element_type=F32)


def _dot_nt(a, b):
    return lax.dot_general(a, b, (((1,), (1,)), ((), ())), preferred_element_type=F32)


def _dot_tn(a, b):
    return lax.dot_general(a, b, (((0,), (0,)), ((), ())), preferred_element_type=F32)


def _log_sigmoid(z):
    return jnp.minimum(z, 0.0) - jnp.log(1.0 + jnp.exp(-jnp.abs(z)))


def _silu(g):
    return g * jax.nn.sigmoid(g)


def _in_proj_kernel(x_ref, nw_ref, w_ref, o_ref, *, n_chunk):
    x = x_ref[...]
    ms = jnp.mean(x * x, axis=-1, keepdims=True)
    h = (x * lax.rsqrt(ms + EPS) * nw_ref[...]).astype(BF16)
    for c in range(o_ref.shape[1] // n_chunk):
        cols = slice(c * n_chunk, (c + 1) * n_chunk)
        o_ref[:, cols] = _dot(h, w_ref[:, cols]).astype(o_ref.dtype)


def _in_proj(x2d, norm_w, w_bf16, *, tm=512, n_chunk=512):
    m, d = x2d.shape
    n = w_bf16.shape[1]
    return pl.pallas_call(
        functools.partial(_in_proj_kernel, n_chunk=n_chunk),
        grid=(m // tm,),
        in_specs=[
            pl.BlockSpec((tm, d), lambda i: (i, 0)),
            pl.BlockSpec((1, d), lambda i: (0, 0)),
            pl.BlockSpec((d, n), lambda i: (0, 0)),
        ],
        out_specs=pl.BlockSpec((tm, n), lambda i: (i, 0)),
        out_shape=jax.ShapeDtypeStruct((m, n), BF16),
        compiler_params=pltpu.CompilerParams(dimension_semantics=("arbitrary",)),
        name="in_proj",
    )(x2d, norm_w.reshape(1, d), w_bf16)


def _out_proj_kernel(ma_ref, mb_ref, w_ref, nw_ref, x_ref, o_ref):
    half = ma_ref.shape[1]
    y = _dot(ma_ref[...], w_ref[:half, :]) + _dot(mb_ref[...], w_ref[half:, :])
    ms = jnp.mean(y * y, axis=-1, keepdims=True)
    o_ref[...] = x_ref[...] + y * lax.rsqrt(ms + EPS) * nw_ref[...]


def _out_proj(mix_a, mix_b, w_bf16, norm_w, x2d, *, tm=512):
    m, d = x2d.shape
    half = mix_a.shape[1]
    return pl.pallas_call(
        _out_proj_kernel,
        grid=(m // tm,),
        in_specs=[
            pl.BlockSpec((tm, half), lambda i: (i, 0)),
            pl.BlockSpec((tm, half), lambda i: (i, 0)),
            pl.BlockSpec((2 * half, d), lambda i: (0, 0)),
            pl.BlockSpec((1, d), lambda i: (0, 0)),
            pl.BlockSpec((tm, d), lambda i: (i, 0)),
        ],
        out_specs=pl.BlockSpec((tm, d), lambda i: (i, 0)),
        out_shape=jax.ShapeDtypeStruct((m, d), F32),
        compiler_params=pltpu.CompilerParams(dimension_semantics=("arbitrary",)),
        name="out_proj",
    )(mix_a, mix_b, w_bf16, norm_w.reshape(1, d), x2d)


def _retention_kernel(q_ref, q2_ref, k_ref, k2_ref, v_ref, g_ref, cos_ref, sin_ref,
                      decay_ref, qd_ref, kd_ref, o_ref, state_ref, *, bb, g_chunk):
    @pl.when(pl.program_id(1) == 0)
    def _():
        state_ref[...] = jnp.zeros_like(state_ref)

    cos = cos_ref[...]
    sin = sin_ref[...]
    lane = lax.broadcasted_iota(jnp.int32, (TOK, LANES), 1)
    head_mask = (lane < RET_DK, lane >= RET_DK)
    units = [(b, h) for b in range(bb) for h in range(RET_HEADS)]
    hl = lambda h: slice(h * LANES, (h + 1) * LANES)
    qm, kp, vs, states = {}, {}, {}, {}
    for b in range(bb):
        qr = q_ref[b].astype(F32) * cos + q2_ref[b].astype(F32) * sin
        kr = (k_ref[b].astype(F32) * cos + k2_ref[b].astype(F32) * sin) * (RET_DK ** -0.5)
        for h in range(RET_HEADS):
            lanes = hl(h // 2)
            qm[b, h] = jnp.where(head_mask[h % 2], qr[:, lanes], 0.0)
            kp[b, h] = kr[:, lanes].astype(BF16)
            vs[b, h] = v_ref[b, :, hl(h)]
            states[b, h] = state_ref[b, h]
    scores = {u: _dot_nt(qm[u].astype(BF16), kp[u]) for u in units}
    inter = {u: _dot((qm[u] * qd_ref[u[1]]).astype(BF16), states[u].astype(BF16)) for u in units}
    kv = {u: _dot_tn(kp[u], (vs[u].astype(F32) * kd_ref[u[1]]).astype(BF16)) for u in units}
    outs = {u: _dot((scores[u] * decay_ref[u[1]]).astype(BF16), vs[u]) + inter[u] for u in units}
    for b, h in units:
        o = outs[b, h]
        state_ref[b, h] = states[b, h] * g_chunk[h] + kv[b, h]
        ms = jnp.mean(o * o, axis=-1, keepdims=True)
        gate = _silu(g_ref[b, :, hl(h)].astype(F32))
        o_ref[b, :, hl(h)] = (o * lax.rsqrt(ms + EPS) * gate).astype(o_ref.dtype)


def _retention_consts(seq):
    h = np.arange(RET_HEADS, dtype=np.float64)
    log_g = np.log(1.0 - 2.0 ** (-5.0 - h))
    pos = np.arange(TOK, dtype=np.float64)
    dist = pos[:, None] - pos[None, :]
    decay = np.where(dist >= 0, np.exp(log_g[:, None, None] * np.maximum(dist, 0.0)), 0.0)
    q_decay = np.exp(log_g[:, None] * (pos + 1.0)[None, :])
    k_decay = np.exp(log_g[:, None] * (TOK - 1.0 - pos)[None, :])
    ones = np.ones((1, 1, LANES))
    g_chunk = tuple(float(v) for v in np.exp(log_g * TOK))
    half = RET_DK // 2
    inv = ROPE_BASE ** (-np.arange(half, dtype=np.float64) / half)
    ang = np.arange(seq, dtype=np.float64)[:, None] * inv[None, :]
    cos = np.tile(np.cos(ang), (1, 2 * RET_HEADS))
    sin = np.tile(np.sin(ang), (1, 2 * RET_HEADS))
    to = lambda a: jnp.asarray(a, dtype=F32)
    return (to(cos), to(sin), to(decay), to(q_decay[:, :, None] * ones),
            to(k_decay[:, :, None] * ones), g_chunk)


def _retention(proj, *, bb=4):
    bsz, seq, _ = proj.shape
    cos, sin, decay, qd, kd, g_chunk = _retention_consts(seq)
    qk_w = RET_HEADS * RET_DK
    v_w = RET_HEADS * RET_DV
    qk_spec = lambda j: pl.BlockSpec((bb, TOK, qk_w), lambda b, c: (b, c, j))
    v_spec = lambda j: pl.BlockSpec((bb, TOK, v_w), lambda b, c: (b, c, j))
    tab_spec = pl.BlockSpec((TOK, qk_w), lambda b, c: (c, 0))
    const_spec = pl.BlockSpec((RET_HEADS, TOK, LANES), lambda b, c: (0, 0, 0))
    return pl.pallas_call(
        functools.partial(_retention_kernel, bb=bb, g_chunk=g_chunk),
        grid=(bsz // bb, seq // TOK),
        in_specs=[qk_spec(0), qk_spec(1), qk_spec(2), qk_spec(3), v_spec(2), v_spec(3),
                  tab_spec, tab_spec, const_spec, const_spec, const_spec],
        out_specs=pl.BlockSpec((bb, TOK, v_w), lambda b, c: (b, c, 0)),
        out_shape=jax.ShapeDtypeStruct((bsz, seq, v_w), BF16),
        scratch_shapes=[pltpu.VMEM((bb, RET_HEADS, LANES, RET_DV), F32)],
        compiler_params=pltpu.CompilerParams(dimension_semantics=("arbitrary", "arbitrary")),
        name="retention",
    )(proj, proj, proj, proj, proj, proj, cos, sin, decay, qd, kd)


SB_NEAR = 3


def _log2_sigmoid_pair(y):
    ny = -y
    soft = jnp.log(1.0 + jnp.exp2(jnp.minimum(y, ny))) * LOG2E
    log_beta = jnp.minimum(y, 0.0) - soft
    return log_beta, log_beta + ny


def _stick_breaking_kernel(q_ref, k_ref, v_ref, g_ref, wtri_ref, tri_ref, o_ref,
                           acc_ref, total_ref, carry_ref):
    i = pl.program_id(1)
    n_pairs = SB_HEADS // 2
    win = SB_NEAR * TOK
    last = (SB_NEAR - 1) * TOK

    def masked_heads(x):
        first = lax.broadcasted_iota(jnp.int32, x.shape, 1) < SB_DH
        zero = jnp.zeros_like(x)
        return jnp.where(first, x, zero), jnp.where(first, zero, x)

    def write_out():
        for p in range(n_pairs):
            lanes = slice(p * LANES, (p + 1) * LANES)
            gate = _silu(g_ref[0, :, lanes].astype(F32))
            o_ref[0, :, lanes] = (acc_ref[p] * gate).astype(o_ref.dtype)

    def near(diag_is_last):
        first_blk = i - (SB_NEAR - 1) if diag_is_last else 0
        rows = pl.ds(pl.multiple_of(first_blk * TOK, TOK), win)
        width = TOK if diag_is_last else win
        row = lax.broadcasted_iota(jnp.int32, (2 * TOK, width), 0) & (TOK - 1)
        col = lax.broadcasted_iota(jnp.int32, (2 * TOK, width), 1)
        strict = (col - row) < (0 if diag_is_last else i * TOK)

        def mask(x):
            if diag_is_last:
                return jnp.concatenate([x[:, :last], jnp.where(strict, x[:, last:], 0.0)], axis=1)
            return jnp.where(strict, x, 0.0)

        pair_lanes = [slice(p * LANES, (p + 1) * LANES) for p in range(n_pairs)]
        qs = [q_ref[0, :, lanes] for lanes in pair_lanes]
        ks = [k_ref[0, rows, lanes] for lanes in pair_lanes]
        vs = [v_ref[0, rows, lanes] for lanes in pair_lanes]
        gates = [g_ref[0, :, lanes] for lanes in pair_lanes]
        tri_cols = [wtri_ref[b * TOK:, b * TOK:(b + 1) * TOK] for b in range(SB_NEAR)]
        ys = [_dot_nt(jnp.concatenate(masked_heads(qs[p]), axis=0), ks[p])
              for p in range(n_pairs)]
        log_betas, suffixes, totals, accs = [], [], [], []
        for p in range(n_pairs):
            log_beta, log_rest = _log2_sigmoid_pair(ys[p])
            log_rest = mask(log_rest).astype(BF16)
            suffix = jnp.concatenate(
                [_dot(log_rest[:, b * TOK:], tri_cols[b]) for b in range(SB_NEAR)], axis=1)
            log_betas.append(log_beta)
            suffixes.append(suffix)
            totals.append(suffix[:, 0:1] + log_rest[:, 0:1].astype(F32))
        for p in range(n_pairs):
            v0, v1 = masked_heads(vs[p])
            w = mask(jnp.exp2(log_betas[p] + suffixes[p])).astype(BF16)
            accs.append(_dot(w[:TOK], v0) + _dot(w[TOK:], v1))
        for p in range(n_pairs):
            acc_ref[p] = accs[p]
            total_ref[p] = totals[p]
            o_ref[0, :, pair_lanes[p]] = (accs[p] * _silu(gates[p].astype(F32))).astype(o_ref.dtype)
        return jnp.max(functools.reduce(jnp.maximum, totals))

    @pl.when(i < SB_NEAR - 1)
    def _():
        near(False)

    @pl.when(i >= SB_NEAR - 1)
    def _():
        top = near(True)

        @pl.when(jnp.logical_and(i >= SB_NEAR, top >= F32_EXP2_UNDERFLOW))
        def _():
            tri = tri_ref[...]
            for p in range(n_pairs):
                for hh in range(2):
                    carry_ref[2 * p + hh] = jnp.broadcast_to(
                        total_ref[p, hh * TOK:(hh + 1) * TOK, :], (TOK, TOK))

            def visit(j):
                blk = pl.ds(pl.multiple_of(j * TOK, TOK), TOK)
                for p in range(n_pairs):
                    lanes = slice(p * LANES, (p + 1) * LANES)
                    qs = masked_heads(q_ref[0, :, lanes])
                    vs = masked_heads(v_ref[0, blk, lanes])
                    kp = k_ref[0, blk, lanes]
                    for hh in range(2):
                        h = 2 * p + hh
                        log_beta, log_rest = _log2_sigmoid_pair(_dot_nt(qs[hh], kp))
                        sums = _dot(log_rest.astype(BF16), tri)
                        carry = carry_ref[h]
                        w = jnp.exp2(log_beta + sums[:, :TOK] + carry)
                        acc_ref[p] += _dot(w.astype(BF16), vs[hh])
                        carry_ref[h] = carry + sums[:, TOK:]

            def cond(state):
                d, far_top = state
                return jnp.logical_and(d <= i, far_top >= F32_EXP2_UNDERFLOW)

            def body(state):
                d, _ = state
                visit(i - d)
                return d + 1, jnp.max(carry_ref[...])

            lax.while_loop(cond, body, (jnp.int32(SB_NEAR), jnp.float32(0.0)))
            write_out()


def _stick_breaking(proj):
    bsz, seq, _ = proj.shape
    w = SB_HEADS * SB_DH
    first = proj.shape[2] // w - 4
    win = SB_NEAR * TOK
    j = np.arange(win)
    later = (j[:, None] > j[None, :]).astype(np.float32)
    tri = np.concatenate([later[:TOK, :TOK], np.ones((TOK, TOK), np.float32)], axis=1)
    blk = lambda c: pl.BlockSpec((1, TOK, w), lambda b, i: (b, i, c))
    full = lambda c: pl.BlockSpec((1, seq, w), lambda b, i: (b, 0, c))
    whole = lambda shape: pl.BlockSpec(shape, lambda b, i: (0,) * len(shape))
    return pl.pallas_call(
        _stick_breaking_kernel,
        grid=(bsz, seq // TOK),
        in_specs=[blk(first), full(first + 1), full(first + 2), blk(first + 3),
                  whole((win, win)), whole((TOK, 2 * TOK))],
        out_specs=pl.BlockSpec((1, TOK, w), lambda b, i: (b, i, 0)),
        out_shape=jax.ShapeDtypeStruct((bsz, seq, w), BF16),
        scratch_shapes=[pltpu.VMEM((SB_HEADS // 2, TOK, LANES), F32),
                        pltpu.VMEM((SB_HEADS // 2, 2 * TOK, 1), F32),
                        pltpu.VMEM((SB_HEADS, TOK, TOK), F32)],
        compiler_params=pltpu.CompilerParams(dimension_semantics=("arbitrary", "arbitrary")),
        name="stick_breaking",
    )(proj, proj, proj, proj, jnp.asarray(later, dtype=BF16), jnp.asarray(tri, dtype=BF16))


HG_LEVELS = tuple(2 ** e for e in range(int(np.log2(TOK))))


def _hgrn_consts():
    t = np.arange(TOK)
    tt, uu = t[:, None], t[None, :]
    mats = [(uu <= tt), (uu > tt)]
    pair_masks = [np.eye(TOK, dtype=bool)]
    for m in HG_LEVELS:
        same_block = (tt // (2 * m)) == (uu // (2 * m))
        up_t = (tt % (2 * m)) >= m
        up_u = (uu % (2 * m)) >= m
        mats.append(same_block & (up_t == up_u) & np.where(up_t, uu <= tt, uu > tt))
        pair_masks.append(same_block & up_t & ~up_u)
    prefix = np.concatenate(mats, axis=0).astype(np.float32)
    return jnp.asarray(prefix, dtype=BF16), jnp.asarray(np.stack(pair_masks), dtype=F32)


def _hgrn_kernel(q_ref, f_ref, i_ref, g_ref, lbl_ref, prefix_ref, pm_ref, o_ref, state_ref,
                 *, bb, layer):
    @pl.when(pl.program_id(1) == 0)
    def _():
        state_ref[...] = jnp.zeros_like(state_ref)

    logits = lbl_ref[...]
    e = jnp.exp(logits - jnp.max(logits, axis=0, keepdims=True))
    soft = e / jnp.sum(e, axis=0, keepdims=True)
    lb_all = jnp.zeros_like(soft[0:1])
    for r in range(1, layer + 1):
        lb_all = lb_all + soft[r:r + 1]
    x1 = jnp.log(lb_all)
    log_keep = jnp.log(1.0 - lb_all)
    prefix = prefix_ref[...]
    n_lv = len(HG_LEVELS)
    units = [(b, h) for b in range(bb) for h in range(HG_HEADS)]
    hl = lambda h: slice(h * LANES, (h + 1) * LANES)
    qs, kks, vs, states, sums = {}, {}, {}, {}, {}
    for b in range(bb):
        for p in range(HG_HEADS // 2):
            lanes = slice(2 * p * LANES, 2 * (p + 1) * LANES)
            fl = f_ref[b, :, lanes].astype(F32)
            ls = _log_sigmoid(fl)
            x2 = log_keep[:, lanes] + ls
            d = x1[:, lanes] - x2
            log_f = jnp.maximum(x1[:, lanes], x2) + jnp.log(1.0 + jnp.exp(jnp.minimum(d, -d)))
            pair_sums = _dot(prefix, (log_f * LOG2E).astype(BF16))
            kk = (1.0 - lb_all[:, lanes]) * jnp.exp(ls - fl)
            for hh in range(2):
                u = (b, 2 * p + hh)
                sums[u] = pair_sums[:, hh * LANES:(hh + 1) * LANES]
                kks[u] = kk[:, hh * LANES:(hh + 1) * LANES]
    for b, h in units:
        qs[b, h] = q_ref[b, :, hl(h)].astype(F32)
        vs[b, h] = i_ref[b, :, hl(h)]
        states[b, h] = state_ref[b, h]
    attns = {}
    for u in units:
        q, kk = qs[u], kks[u]
        attn = _dot_nt(q.astype(BF16), kk.astype(BF16)) * pm_ref[0]
        for lv in range(n_lv):
            dec = jnp.exp2(sums[u][(2 + lv) * TOK:(3 + lv) * TOK])
            attn = attn + _dot_nt((q * dec).astype(BF16), (kk * dec).astype(BF16)) * pm_ref[lv + 1]
        attns[u] = attn.astype(BF16)
    outs, kvs, cums = {}, {}, {}
    for u in units:
        cums[u] = sums[u][0:TOK]
        ktail = (kks[u] * jnp.exp2(sums[u][TOK:2 * TOK])).astype(BF16)
        q_in = (qs[u] * jnp.exp2(cums[u])).astype(BF16)
        outs[u] = _dot(attns[u], vs[u]) + _dot_nt(q_in, states[u].astype(BF16))
        kvs[u] = _dot_tn(vs[u], ktail)
    for b, h in units:
        o = outs[b, h]
        state_ref[b, h] = states[b, h] * jnp.exp2(cums[b, h][TOK - 1:TOK, :]) + kvs[b, h]
        ms = jnp.mean(o * o, axis=-1, keepdims=True)
        gate = _silu(g_ref[b, :, hl(h)].astype(F32))
        o_ref[b, :, hl(h)] = (o * lax.rsqrt(ms + EPS) * gate).astype(o_ref.dtype)


def _hgrn(proj, lb_logits, layer, *, bb=4):
    bsz, seq, _ = proj.shape
    w = HG_HEADS * LANES
    prefix, pair_masks = _hgrn_consts()
    blk = lambda c: pl.BlockSpec((bb, TOK, w), lambda b, i: (b, i, c))
    whole = lambda a: pl.BlockSpec(a.shape, lambda b, i: (0,) * a.ndim)
    return pl.pallas_call(
        functools.partial(_hgrn_kernel, bb=bb, layer=layer),
        grid=(bsz // bb, seq // TOK),
        in_specs=[blk(0), blk(1), blk(2), blk(3), whole(lb_logits), whole(prefix),
                  whole(pair_masks)],
        out_specs=pl.BlockSpec((bb, TOK, w), lambda b, i: (b, i, 0)),
        out_shape=jax.ShapeDtypeStruct((bsz, seq, w), BF16),
        scratch_shapes=[pltpu.VMEM((bb, HG_HEADS, LANES, LANES), F32)],
        compiler_params=pltpu.CompilerParams(dimension_semantics=("arbitrary", "arbitrary")),
        name="hgrn2",
    )(proj, proj, proj, proj, lb_logits, prefix, pair_masks)


LRU_TOK = 256
HALO = 8


def _lru_kernel(x_ref, g_ref, cw_ref, cb_ref, wa_ref, ba_ref, wx_ref, bx_ref, lam_ref, o_ref,
                xbuf_ref, h_ref, *, bb):
    c = pl.program_id(1)
    tok = x_ref.shape[1]

    @pl.when(c == 0)
    def _():
        xbuf_ref[...] = jnp.zeros_like(xbuf_ref)
        h_ref[...] = jnp.zeros_like(h_ref)

    lam = lam_ref[...]
    neg_sp = -(jnp.maximum(-lam, 0.0) + jnp.log(1.0 + jnp.exp(-jnp.abs(lam))))
    row = lax.broadcasted_iota(jnp.int32, (tok, LRU_WIDTH), 0)
    first_token = jnp.logical_and(row == 0, c == 0)
    sub_row = lax.broadcasted_iota(jnp.int32, (tok // SUBLANES, SUBLANES, LRU_WIDTH), 1)
    groups = LRU_WIDTH // LANES
    for b in range(bb):
        x = x_ref[b].astype(F32)
        xbuf_ref[b, HALO:HALO + tok, :] = x
        y = cb_ref[...] + x * cw_ref[CONV_W - 1:CONV_W, :]
        for j in range(CONV_W - 1):
            shift = CONV_W - 1 - j
            y = y + xbuf_ref[b, HALO - shift:HALO - shift + tok, :] * cw_ref[j:j + 1, :]
        xbuf_ref[b, 0:HALO, :] = x[tok - HALO:tok, :]
        ra, rx = [], []
        for gidx in range(groups):
            yg = y[:, gidx * LANES:(gidx + 1) * LANES].astype(BF16)
            ra.append(_dot(yg, wa_ref[gidx]))
            rx.append(_dot(yg, wx_ref[gidx]))
        r = jax.nn.sigmoid(jnp.concatenate(ra, axis=1) + ba_ref[...])
        ig = jax.nn.sigmoid(jnp.concatenate(rx, axis=1) + bx_ref[...])
        log_a = LRU_C * r * neg_sp
        a = jnp.exp(log_a)
        mult = jnp.sqrt(-jnp.tanh(log_a) * (1.0 + a * a))
        mult = jnp.where(first_token, 1.0, mult)
        u = mult * ig * y
        a = a.reshape(tok // SUBLANES, SUBLANES, LRU_WIDTH)
        u = u.reshape(tok // SUBLANES, SUBLANES, LRU_WIDTH)
        d = 1
        while d < SUBLANES:
            keep = sub_row >= d
            a_prev = jnp.where(keep, pltpu.roll(a, d, 1), 1.0)
            u_prev = jnp.where(keep, pltpu.roll(u, d, 1), 0.0)
            u = a * u_prev + u
            a = a * a_prev
            d *= 2
        h_in = h_ref[b]
        hs = []
        for j in range(tok // SUBLANES):
            hs.append(u[j] + a[j] * h_in)
            h_in = hs[-1][SUBLANES - 1:SUBLANES, :]
        h_ref[b] = h_in
        hcur = jnp.concatenate(hs, axis=0)
        o_ref[b] = (hcur * _silu(g_ref[b].astype(F32))).astype(o_ref.dtype)


def _block_diag_pairs(w):
    n, bw, _ = w.shape
    z = jnp.zeros((n // 2, bw, bw), w.dtype)
    top = jnp.concatenate([w[0::2], z], axis=2)
    bot = jnp.concatenate([z, w[1::2]], axis=2)
    return jnp.concatenate([top, bot], axis=1).astype(BF16)


def _lru(proj, conv_w, conv_b, w_a, b_a, w_x, b_x, lam, *, bb=2):
    bsz, seq, _ = proj.shape
    w = LRU_WIDTH
    first = proj.shape[2] // w - 2
    row = lambda a: a.reshape(1, w).astype(F32)
    blk = lambda c: pl.BlockSpec((bb, LRU_TOK, w), lambda b, i: (b, i, c))
    whole = lambda shape: pl.BlockSpec(shape, lambda b, i: (0,) * len(shape))
    groups = w // LANES
    return pl.pallas_call(
        functools.partial(_lru_kernel, bb=bb),
        grid=(bsz // bb, seq // LRU_TOK),
        in_specs=[blk(first), blk(first + 1), whole((CONV_W, w)), whole((1, w)),
                  whole((groups, LANES, LANES)), whole((1, w)),
                  whole((groups, LANES, LANES)), whole((1, w)), whole((1, w))],
        out_specs=pl.BlockSpec((bb, LRU_TOK, w), lambda b, i: (b, i, 0)),
        out_shape=jax.ShapeDtypeStruct((bsz, seq, w), BF16),
        scratch_shapes=[pltpu.VMEM((bb, HALO + LRU_TOK, w), F32), pltpu.VMEM((bb, 1, w), F32)],
        compiler_params=pltpu.CompilerParams(dimension_semantics=("arbitrary", "arbitrary")),
        name="rg_lru",
    )(proj, proj, conv_w.astype(F32), row(conv_b), _block_diag_pairs(w_a), row(b_a),
      _block_diag_pairs(w_x), row(b_x), row(lam))


def _rotate_half_columns(w, heads, dk):
    d = w.shape[0]
    wh = w.reshape(d, heads, dk)
    half = dk // 2
    return jnp.concatenate([-wh[..., half:], wh[..., :half]], axis=-1).reshape(d, heads * dk)


def _even_in_weights(w_in):
    qk = RET_HEADS * RET_DK
    wq, wk, rest = w_in[:, :qk], w_in[:, qk:2 * qk], w_in[:, 2 * qk:]
    sq = slice(2 * RET_HEADS * RET_DV, 2 * RET_HEADS * RET_DV + SB_HEADS * SB_DH)
    rest = rest.at[:, sq].multiply(SB_DH ** -0.5 * LOG2E)
    cols = [wq, _rotate_half_columns(wq, RET_HEADS, RET_DK),
            wk, _rotate_half_columns(wk, RET_HEADS, RET_DK), rest]
    return jnp.concatenate(cols, axis=1).astype(BF16)


def kernel(x, pre_norm_w, post_norm_w, even_w_in, even_w_out, odd_w_in, odd_w_out, hgrn_lb_logits,
           conv_w, conv_b, lru_w_a, lru_b_a, lru_w_x, lru_b_x, lru_lambda):
    bsz, seq, d = x.shape
    depth = pre_norm_w.shape[0]
    x2d = x.reshape(bsz * seq, d)
    for layer in range(depth):
        idx = layer // 2
        if layer % 2 == 0:
            proj = _in_proj(x2d, pre_norm_w[layer], _even_in_weights(even_w_in[idx]))
            proj = proj.reshape(bsz, seq, -1)
            mix_a = _retention(proj)
            mix_b = _stick_breaking(proj)
            w_out = even_w_out[idx]
        else:
            proj = _in_proj(x2d, pre_norm_w[layer], odd_w_in[idx].astype(BF16))
            proj = proj.reshape(bsz, seq, -1)
            mix_a = _hgrn(proj, hgrn_lb_logits.astype(F32), idx)
            mix_b = _lru(proj, conv_w[idx], conv_b[idx], lru_w_a[idx], lru_b_a[idx],
                         lru_w_x[idx], lru_b_x[idx], lru_lambda[idx])
            w_out = odd_w_out[idx]
        x2d = _out_proj(mix_a.reshape(bsz * seq, -1), mix_b.reshape(bsz * seq, -1),
                        w_out.astype(BF16), post_norm_w[layer], x2d)
    return x2d.reshape(bsz, seq, d)
```

```python
import functools

import numpy as np
import jax
import jax.numpy as jnp
from jax import lax
from jax.experimental import pallas as pl
from jax.experimental.pallas import tpu as pltpu

F32 = jnp.float32
BF16 = jnp.bfloat16

EPS = 1e-6
LANES = 128
SUBLANES = 8
TOK = 128
ROPE_BASE = 10000.0
RET_HEADS, RET_DK, RET_DV = 4, 64, 128
SB_HEADS, SB_DH = 8, 64
HG_HEADS = 4
LRU_WIDTH, LRU_BLOCKS, CONV_W, LRU_C = 512, 8, 4, 8.0
LOG2E = float(np.log2(np.e))
F32_EXP2_UNDERFLOW = -150.0


def _dot(a, b):
    return jnp.dot(a, b, preferred_element_type=F32)


def _dot_nt(a, b):
    return lax.dot_general(a, b, (((1,), (1,)), ((), ())), preferred_element_type=F32)


def _dot_tn(a, b):
    return lax.dot_general(a, b, (((0,), (0,)), ((), ())), preferred_element_type=F32)


def _log_sigmoid(z):
    return jnp.minimum(z, 0.0) - jnp.log(1.0 + jnp.exp(-jnp.abs(z)))


def _silu(g):
    return g * jax.nn.sigmoid(g)


def _in_proj_kernel(x_ref, nw_ref, w_ref, o_ref, *, n_chunk):
    x = x_ref[...]
    ms = jnp.mean(x * x, axis=-1, keepdims=True)
    h = (x * lax.rsqrt(ms + EPS) * nw_ref[...]).astype(BF16)
    for c in range(o_ref.shape[1] // n_chunk):
        cols = slice(c * n_chunk, (c + 1) * n_chunk)
        o_ref[:, cols] = _dot(h, w_ref[:, cols]).astype(o_ref.dtype)


def _in_proj(x2d, norm_w, w_bf16, *, tm=512, n_chunk=512):
    m, d = x2d.shape
    n = w_bf16.shape[1]
    return pl.pallas_call(
        functools.partial(_in_proj_kernel, n_chunk=n_chunk),
        grid=(m // tm,),
        in_specs=[
            pl.BlockSpec((tm, d), lambda i: (i, 0)),
            pl.BlockSpec((1, d), lambda i: (0, 0)),
            pl.BlockSpec((d, n), lambda i: (0, 0)),
        ],
        out_specs=pl.BlockSpec((tm, n), lambda i: (i, 0)),
        out_shape=jax.ShapeDtypeStruct((m, n), BF16),
        compiler_params=pltpu.CompilerParams(dimension_semantics=("arbitrary",)),
        name="in_proj",
    )(x2d, norm_w.reshape(1, d), w_bf16)


def _out_proj_kernel(ma_ref, mb_ref, w_ref, nw_ref, x_ref, o_ref):
    half = ma_ref.shape[1]
    y = _dot(ma_ref[...], w_ref[:half, :]) + _dot(mb_ref[...], w_ref[half:, :])
    ms = jnp.mean(y * y, axis=-1, keepdims=True)
    o_ref[...] = x_ref[...] + y * lax.rsqrt(ms + EPS) * nw_ref[...]


def _out_proj(mix_a, mix_b, w_bf16, norm_w, x2d, *, tm=512):
    m, d = x2d.shape
    half = mix_a.shape[1]
    return pl.pallas_call(
        _out_proj_kernel,
        grid=(m // tm,),
        in_specs=[
            pl.BlockSpec((tm, half), lambda i: (i, 0)),
            pl.BlockSpec((tm, half), lambda i: (i, 0)),
            pl.BlockSpec((2 * half, d), lambda i: (0, 0)),
            pl.BlockSpec((1, d), lambda i: (0, 0)),
            pl.BlockSpec((tm, d), lambda i: (i, 0)),
        ],
        out_specs=pl.BlockSpec((tm, d), lambda i: (i, 0)),
        out_shape=jax.ShapeDtypeStruct((m, d), F32),
        compiler_params=pltpu.CompilerParams(dimension_semantics=("arbitrary",)),
        name="out_proj",
    )(mix_a, mix_b, w_bf16, norm_w.reshape(1, d), x2d)


def _retention_kernel(q_ref, q2_ref, k_ref, k2_ref, v_ref, g_ref, cos_ref, sin_ref,
                      decay_ref, qd_ref, kd_ref, o_ref, state_ref, *, bb, g_chunk):
    @pl.when(pl.program_id(1) == 0)
    def _():
        state_ref[...] = jnp.zeros_like(state_ref)

    cos = cos_ref[...]
    sin = sin_ref[...]
    lane = lax.broadcasted_iota(jnp.int32, (TOK, LANES), 1)
    head_mask = (lane < RET_DK, lane >= RET_DK)
    units = [(b, h) for b in range(bb) for h in range(RET_HEADS)]
    hl = lambda h: slice(h * LANES, (h + 1) * LANES)
    qm, kp, vs, states = {}, {}, {}, {}
    for b in range(bb):
        qr = q_ref[b].astype(F32) * cos + q2_ref[b].astype(F32) * sin
        kr = (k_ref[b].astype(F32) * cos + k2_ref[b].astype(F32) * sin) * (RET_DK ** -0.5)
        for h in range(RET_HEADS):
            lanes = hl(h // 2)
            qm[b, h] = jnp.where(head_mask[h % 2], qr[:, lanes], 0.0)
            kp[b, h] = kr[:, lanes].astype(BF16)
            vs[b, h] = v_ref[b, :, hl(h)]
            states[b, h] = state_ref[b, h]
    scores = {u: _dot_nt(qm[u].astype(BF16), kp[u]) for u in units}
    inter = {u: _dot((qm[u] * qd_ref[u[1]]).astype(BF16), states[u].astype(BF16)) for u in units}
    kv = {u: _dot_tn(kp[u], (vs[u].astype(F32) * kd_ref[u[1]]).astype(BF16)) for u in units}
    outs = {u: _dot((scores[u] * decay_ref[u[1]]).astype(BF16), vs[u]) + inter[u] for u in units}
    for b, h in units:
        o = outs[b, h]
        state_ref[b, h] = states[b, h] * g_chunk[h] + kv[b, h]
        ms = jnp.mean(o * o, axis=-1, keepdims=True)
        gate = _silu(g_ref[b, :, hl(h)].astype(F32))
        o_ref[b, :, hl(h)] = (o * lax.rsqrt(ms + EPS) * gate).astype(o_ref.dtype)


def _retention_consts(seq):
    h = np.arange(RET_HEADS, dtype=np.float64)
    log_g = np.log(1.0 - 2.0 ** (-5.0 - h))
    pos = np.arange(TOK, dtype=np.float64)
    dist = pos[:, None] - pos[None, :]
    decay = np.where(dist >= 0, np.exp(log_g[:, None, None] * np.maximum(dist, 0.0)), 0.0)
    q_decay = np.exp(log_g[:, None] * (pos + 1.0)[None, :])
    k_decay = np.exp(log_g[:, None] * (TOK - 1.0 - pos)[None, :])
    ones = np.ones((1, 1, LANES))
    g_chunk = tuple(float(v) for v in np.exp(log_g * TOK))
    half = RET_DK // 2
    inv = ROPE_BASE ** (-np.arange(half, dtype=np.float64) / half)
    ang = np.arange(seq, dtype=np.float64)[:, None] * inv[None, :]
    cos = np.tile(np.cos(ang), (1, 2 * RET_HEADS))
    sin = np.tile(np.sin(ang), (1, 2 * RET_HEADS))
    to = lambda a: jnp.asarray(a, dtype=F32)
    return (to(cos), to(sin), to(decay), to(q_decay[:, :, None] * ones),
            to(k_decay[:, :, None] * ones), g_chunk)


def _retention(proj, *, bb=4):
    bsz, seq, _ = proj.shape
    cos, sin, decay, qd, kd, g_chunk = _retention_consts(seq)
    qk_w = RET_HEADS * RET_DK
    v_w = RET_HEADS * RET_DV
    qk_spec = lambda j: pl.BlockSpec((bb, TOK, qk_w), lambda b, c: (b, c, j))
    v_spec = lambda j: pl.BlockSpec((bb, TOK, v_w), lambda b, c: (b, c, j))
    tab_spec = pl.BlockSpec((TOK, qk_w), lambda b, c: (c, 0))
    const_spec = pl.BlockSpec((RET_HEADS, TOK, LANES), lambda b, c: (0, 0, 0))
    return pl.pallas_call(
        functools.partial(_retention_kernel, bb=bb, g_chunk=g_chunk),
        grid=(bsz // bb, seq // TOK),
        in_specs=[qk_spec(0), qk_spec(1), qk_spec(2), qk_spec(3), v_spec(2), v_spec(3),
                  tab_spec, tab_spec, const_spec, const_spec, const_spec],
        out_specs=pl.BlockSpec((bb, TOK, v_w), lambda b, c: (b, c, 0)),
        out_shape=jax.ShapeDtypeStruct((bsz, seq, v_w), BF16),
        scratch_shapes=[pltpu.VMEM((bb, RET_HEADS, LANES, RET_DV), F32)],
        compiler_params=pltpu.CompilerParams(dimension_semantics=("arbitrary", "arbitrary")),
        name="retention",
    )(proj, proj, proj, proj, proj, proj, cos, sin, decay, qd, kd)


SB_NEAR = 3
SB_SUB = 64
SB_KEYS = 256


def _log2_sigmoid_pair(y):
    soft = jnp.log(1.0 + jnp.exp2(-jnp.abs(y))) * LOG2E
    log_beta = jnp.minimum(y, 0.0) - soft
    return log_beta, log_beta - y


def _stick_breaking_kernel(q_ref, k_ref, v_ref, g_ref, wtri_ref, tri_ref, o_ref,
                           acc_ref, total_ref, carry_ref, *, bb):
    i = pl.program_id(1)
    n_pairs = SB_HEADS // 2
    win = SB_NEAR * TOK
    pair_lanes = [slice(p * LANES, (p + 1) * LANES) for p in range(n_pairs)]
    units = [(b, p) for b in range(bb) for p in range(n_pairs)]

    def pick_head(from_first, from_second):
        first = lax.broadcasted_iota(jnp.int32, from_first.shape, 1) < SB_DH
        return jnp.where(first, from_first, from_second)

    def masked_heads(x):
        first = lax.broadcasted_iota(jnp.int32, x.shape, 1) < SB_DH
        zero = jnp.zeros_like(x)
        return jnp.where(first, x, zero), jnp.where(first, zero, x)

    def write_out():
        for b, p in units:
            gate = _silu(g_ref[b, :, pair_lanes[p]].astype(F32))
            o_ref[b, :, pair_lanes[p]] = (acc_ref[b, p] * gate).astype(o_ref.dtype)

    def finish(accs, totals):
        gates = {(b, p): g_ref[b, :, pair_lanes[p]] for b, p in units}
        for b, p in units:
            acc_ref[b, p] = accs[b, p]
            total_ref[b, p] = totals[b, p]
            o_ref[b, :, pair_lanes[p]] = (
                accs[b, p] * _silu(gates[b, p].astype(F32))).astype(o_ref.dtype)
        return jnp.max(functools.reduce(jnp.maximum, totals.values()))


    def near_first_blocks():
        row = lax.broadcasted_iota(jnp.int32, (2 * TOK, win), 0) & (TOK - 1)
        col = lax.broadcasted_iota(jnp.int32, (2 * TOK, win), 1)
        strict = (col - row) < i * TOK
        qs = {(b, p): q_ref[b, :, pair_lanes[p]] for b, p in units}
        ks = {(b, p): k_ref[b, 0:win, pair_lanes[p]] for b, p in units}
        vs = {(b, p): v_ref[b, 0:win, pair_lanes[p]] for b, p in units}
        wtri = wtri_ref[...]
        ys = {u: _dot_nt(jnp.concatenate(masked_heads(qs[u]), axis=0), ks[u]) for u in units}
        log_betas, suffixes, totals, accs = {}, {}, {}, {}
        for u in units:
            log_betas[u], log_rest = _log2_sigmoid_pair(ys[u])
            log_rest = jnp.where(strict, log_rest, 0.0).astype(BF16)
            suffixes[u] = _dot(log_rest, wtri)
            totals[u] = suffixes[u][:, 0:1] + log_rest[:, 0:1].astype(F32)
        for u in units:
            w = jnp.where(strict, jnp.exp2(log_betas[u] + suffixes[u]), 0.0).astype(BF16)
            accs[u] = pick_head(_dot(w[:TOK], vs[u]), _dot(w[TOK:], vs[u]))
        return finish(accs, totals)

    def near():
        subs = TOK // SB_SUB
        rho = lax.broadcasted_iota(jnp.int32, (2 * SB_SUB, LANES), 0) & (SB_SUB - 1)
        col = lax.broadcasted_iota(jnp.int32, (2 * SB_SUB, LANES), 1)
        strict = col < rho + (LANES - SB_SUB)

        def mask(x):
            body = SB_KEYS - LANES
            return jnp.concatenate([x[:, :body], jnp.where(strict, x[:, body:], 0.0)], axis=1)

        tiles = [(b, p, s) for b, p in units for s in range(subs)]
        tri = wtri_ref[0:SB_KEYS, 0:SB_KEYS]
        lhs, ks, vs = {}, {}, {}
        for b, p in units:
            q0, q1 = masked_heads(q_ref[b, :, pair_lanes[p]])
            for s in range(subs):
                sub = slice(s * SB_SUB, (s + 1) * SB_SUB)
                start = pl.multiple_of((i * TOK + (s + 1) * SB_SUB) - SB_KEYS, SB_SUB)
                lhs[b, p, s] = jnp.concatenate([q0[sub], q1[sub]], axis=0)
                ks[b, p, s] = k_ref[b, pl.ds(start, SB_KEYS), pair_lanes[p]]
                vs[b, p, s] = v_ref[b, pl.ds(start, SB_KEYS), pair_lanes[p]]
        ys = {t: _dot_nt(lhs[t], ks[t]) for t in tiles}
        log_betas, suffixes, sub_totals, sub_accs = {}, {}, {}, {}
        for t in tiles:
            log_betas[t], log_rest = _log2_sigmoid_pair(ys[t])
            log_rest = mask(log_rest).astype(BF16)
            suffixes[t] = _dot(log_rest, tri)
            sub_totals[t] = suffixes[t][:, 0:1] + log_rest[:, 0:1].astype(F32)
        for t in tiles:
            w = mask(jnp.exp2(log_betas[t] + suffixes[t])).astype(BF16)
            sub_accs[t] = pick_head(_dot(w[:SB_SUB], vs[t]), _dot(w[SB_SUB:], vs[t]))
        accs = {(b, p): jnp.concatenate([sub_accs[b, p, s] for s in range(subs)], axis=0)
                for b, p in units}
        totals = {(b, p): jnp.concatenate(
            [sub_totals[b, p, s][hh * SB_SUB:(hh + 1) * SB_SUB]
             for hh in range(2) for s in range(subs)], axis=0) for b, p in units}
        return finish(accs, totals)

    @pl.when(i < SB_NEAR - 1)
    def _():
        near_first_blocks()

    @pl.when(i >= SB_NEAR - 1)
    def _():
        top = near()

        @pl.when(top >= F32_EXP2_UNDERFLOW)
        def _():
            tri = tri_ref[...]
            for b, p in units:
                for hh in range(2):
                    carry_ref[b, 2 * p + hh] = jnp.broadcast_to(
                        total_ref[b, p, hh * TOK:(hh + 1) * TOK, :], (TOK, TOK))
            row = lax.broadcasted_iota(jnp.int32, (TOK, TOK), 0)
            col = lax.broadcasted_iota(jnp.int32, (TOK, TOK), 1)
            uncovered = col < (row // SB_SUB + 1) * SB_SUB + (SB_NEAR - 1) * TOK - SB_KEYS

            def visit(j, partly_covered):
                blk = pl.ds(pl.multiple_of(j * TOK, TOK), TOK)
                for b, p in units:
                    qs = masked_heads(q_ref[b, :, pair_lanes[p]])
                    vs = masked_heads(v_ref[b, blk, pair_lanes[p]])
                    kp = k_ref[b, blk, pair_lanes[p]]
                    for hh in range(2):
                        h = 2 * p + hh
                        log_beta, log_rest = _log2_sigmoid_pair(_dot_nt(qs[hh], kp))
                        if partly_covered:
                            log_rest = jnp.where(uncovered, log_rest, 0.0)
                        sums = _dot(log_rest.astype(BF16), tri)
                        carry = carry_ref[b, h]
                        w = jnp.exp2(log_beta + sums[:, :TOK] + carry)
                        if partly_covered:
                            w = jnp.where(uncovered, w, 0.0)
                        acc_ref[b, p] += _dot(w.astype(BF16), vs[hh])
                        carry_ref[b, h] = carry + sums[:, TOK:]

            visit(i - (SB_NEAR - 1), True)

            def cond(state):
                d, far_top = state
                return jnp.logical_and(d <= i, far_top >= F32_EXP2_UNDERFLOW)

            def body(state):
                d, _ = state
                visit(i - d, False)
                return d + 1, jnp.max(carry_ref[...])

            lax.while_loop(cond, body, (jnp.int32(SB_NEAR), jnp.max(carry_ref[...])))
            write_out()


def _stick_breaking(proj, *, bb=2):
    bsz, seq, _ = proj.shape
    w = SB_HEADS * SB_DH
    first = proj.shape[2] // w - 4
    win = SB_NEAR * TOK
    j = np.arange(win)
    later = (j[:, None] > j[None, :]).astype(np.float32)
    tri = np.concatenate([later[:TOK, :TOK], np.ones((TOK, TOK), np.float32)], axis=1)
    blk = lambda c: pl.BlockSpec((bb, TOK, w), lambda b, i: (b, i, c))
    full = lambda c: pl.BlockSpec((bb, seq, w), lambda b, i: (b, 0, c))
    whole = lambda shape: pl.BlockSpec(shape, lambda b, i: (0,) * len(shape))
    return pl.pallas_call(
        functools.partial(_stick_breaking_kernel, bb=bb),
        grid=(bsz // bb, seq // TOK),
        in_specs=[blk(first), full(first + 1), full(first + 2), blk(first + 3),
                  whole((win, win)), whole((TOK, 2 * TOK))],
        out_specs=pl.BlockSpec((bb, TOK, w), lambda b, i: (b, i, 0)),
        out_shape=jax.ShapeDtypeStruct((bsz, seq, w), BF16),
        scratch_shapes=[pltpu.VMEM((bb, SB_HEADS // 2, TOK, LANES), F32),
                        pltpu.VMEM((bb, SB_HEADS // 2, 2 * TOK, 1), F32),
                        pltpu.VMEM((bb, SB_HEADS, TOK, TOK), F32)],
        compiler_params=pltpu.CompilerParams(dimension_semantics=("arbitrary", "arbitrary")),
        name="stick_breaking",
    )(proj, proj, proj, proj, jnp.asarray(later, dtype=BF16), jnp.asarray(tri, dtype=BF16))


HG_LEVELS = tuple(2 ** e for e in range(int(np.log2(TOK))))
HG_FIRST_VPU_LEVEL = SUBLANES // 2


def _half_boundary(cum3, last_rows, m):
    if 2 * m == SUBLANES:
        return jnp.broadcast_to(cum3[:, m - 1:m, :], cum3.shape)
    per_block = 2 * m // SUBLANES
    picks = [(g // per_block) * per_block + per_block // 2 - 1 for g in range(cum3.shape[0])]
    return jnp.concatenate([last_rows[g:g + 1] for g in picks], axis=0)


def _hgrn_consts():
    t = np.arange(TOK)
    tt, uu = t[:, None], t[None, :]
    mats = [(uu <= tt)]
    pair_masks = [np.eye(TOK, dtype=bool)]
    for m in HG_LEVELS:
        same_block = (tt // (2 * m)) == (uu // (2 * m))
        up_t = (tt % (2 * m)) >= m
        up_u = (uu % (2 * m)) >= m
        if m < HG_FIRST_VPU_LEVEL:
            mats.append(same_block & (up_t == up_u) & np.where(up_t, uu <= tt, uu > tt))
        pair_masks.append(same_block & up_t & ~up_u)
    prefix = np.concatenate(mats, axis=0).astype(np.float32)
    return jnp.asarray(prefix, dtype=BF16), jnp.asarray(np.stack(pair_masks), dtype=F32)


def _hgrn_kernel(q_ref, f_ref, i_ref, g_ref, lbl_ref, prefix_ref, pm_ref, o_ref, state_ref,
                 *, bb, layer):
    @pl.when(pl.program_id(1) == 0)
    def _():
        state_ref[...] = jnp.zeros_like(state_ref)

    logits = lbl_ref[...]
    e = jnp.exp(logits - jnp.max(logits, axis=0, keepdims=True))
    soft = e / jnp.sum(e, axis=0, keepdims=True)
    lb_all = jnp.zeros_like(soft[0:1])
    for r in range(1, layer + 1):
        lb_all = lb_all + soft[r:r + 1]
    x1 = jnp.log(lb_all)
    log_keep = jnp.log(1.0 - lb_all)
    prefix = prefix_ref[...]
    n_lv = len(HG_LEVELS)
    units = [(b, h) for b in range(bb) for h in range(HG_HEADS)]
    hl = lambda h: slice(h * LANES, (h + 1) * LANES)
    qs, kks, vs, states, cums, rests, level_sums = {}, {}, {}, {}, {}, {}, {}
    for b in range(bb):
        for p in range(HG_HEADS // 2):
            lanes = slice(2 * p * LANES, 2 * (p + 1) * LANES)
            fl = f_ref[b, :, lanes].astype(F32)
            ls = _log_sigmoid(fl)
            x2 = log_keep[:, lanes] + ls
            d = x1[:, lanes] - x2
            log_f = jnp.maximum(x1[:, lanes], x2) + jnp.log(1.0 + jnp.exp(jnp.minimum(d, -d)))
            pair_sums = _dot(prefix, (log_f * LOG2E).astype(BF16))
            cum = pair_sums[0:TOK]
            levels = [pair_sums[(1 + lv) * TOK:(2 + lv) * TOK]
                      for lv in range(n_lv) if HG_LEVELS[lv] < HG_FIRST_VPU_LEVEL]
            cum3 = cum.reshape(TOK // SUBLANES, SUBLANES, 2 * LANES)
            last_rows = jnp.broadcast_to(cum3[:, SUBLANES - 1:SUBLANES, :], cum3.shape)
            for m in HG_LEVELS[len(levels):]:
                gap = cum3 - _half_boundary(cum3, last_rows, m)
                levels.append(jnp.minimum(gap, -gap).reshape(TOK, 2 * LANES))
            rest = cum[TOK - 1:TOK, :] - cum
            kk = (1.0 - lb_all[:, lanes]) * jnp.exp(ls - fl)
            for hh in range(2):
                u = (b, 2 * p + hh)
                head = slice(hh * LANES, (hh + 1) * LANES)
                cums[u], rests[u], kks[u] = cum[:, head], rest[:, head], kk[:, head]
                level_sums[u] = [x[:, head] for x in levels]
    for b, h in units:
        qs[b, h] = q_ref[b, :, hl(h)].astype(F32)
        vs[b, h] = i_ref[b, :, hl(h)]
        states[b, h] = state_ref[b, h]
    attns = {}
    for u in units:
        q, kk = qs[u], kks[u]
        attn = _dot_nt(q.astype(BF16), kk.astype(BF16)) * pm_ref[0]
        for lv in range(n_lv):
            dec = jnp.exp2(level_sums[u][lv])
            attn = attn + _dot_nt((q * dec).astype(BF16), (kk * dec).astype(BF16)) * pm_ref[lv + 1]
        attns[u] = attn.astype(BF16)
    outs, kvs = {}, {}
    for u in units:
        ktail = (kks[u] * jnp.exp2(rests[u])).astype(BF16)
        q_in = (qs[u] * jnp.exp2(cums[u])).astype(BF16)
        outs[u] = _dot(attns[u], vs[u]) + _dot_nt(q_in, states[u].astype(BF16))
        kvs[u] = _dot_tn(vs[u], ktail)
    for b, h in units:
        o = outs[b, h]
        state_ref[b, h] = states[b, h] * jnp.exp2(cums[b, h][TOK - 1:TOK, :]) + kvs[b, h]
        ms = jnp.mean(o * o, axis=-1, keepdims=True)
        gate = _silu(g_ref[b, :, hl(h)].astype(F32))
        o_ref[b, :, hl(h)] = (o * lax.rsqrt(ms + EPS) * gate).astype(o_ref.dtype)


def _hgrn(proj, lb_logits, layer, *, bb=4):
    bsz, seq, _ = proj.shape
    w = HG_HEADS * LANES
    prefix, pair_masks = _hgrn_consts()
    blk = lambda c: pl.BlockSpec((bb, TOK, w), lambda b, i: (b, i, c))
    whole = lambda a: pl.BlockSpec(a.shape, lambda b, i: (0,) * a.ndim)
    return pl.pallas_call(
        functools.partial(_hgrn_kernel, bb=bb, layer=layer),
        grid=(bsz // bb, seq // TOK),
        in_specs=[blk(0), blk(1), blk(2), blk(3), whole(lb_logits), whole(prefix),
                  whole(pair_masks)],
        out_specs=pl.BlockSpec((bb, TOK, w), lambda b, i: (b, i, 0)),
        out_shape=jax.ShapeDtypeStruct((bsz, seq, w), BF16),
        scratch_shapes=[pltpu.VMEM((bb, HG_HEADS, LANES, LANES), F32)],
        compiler_params=pltpu.CompilerParams(dimension_semantics=("arbitrary", "arbitrary")),
        name="hgrn2",
    )(proj, proj, proj, proj, lb_logits, prefix, pair_masks)


LRU_TOK = 256
HALO = 8


def _lru_kernel(x_ref, g_ref, cw_ref, cb_ref, wa_ref, ba_ref, wx_ref, bx_ref, lam_ref, o_ref,
                xbuf_ref, h_ref, *, bb):
    c = pl.program_id(1)
    tok = x_ref.shape[1]

    @pl.when(c == 0)
    def _():
        xbuf_ref[...] = jnp.zeros_like(xbuf_ref)
        h_ref[...] = jnp.zeros_like(h_ref)

    lam = lam_ref[...]
    neg_sp = -(jnp.maximum(-lam, 0.0) + jnp.log(1.0 + jnp.exp(-jnp.abs(lam))))
    row = lax.broadcasted_iota(jnp.int32, (tok, LRU_WIDTH), 0)
    first_token = jnp.logical_and(row == 0, c == 0)
    sub_row = lax.broadcasted_iota(jnp.int32, (tok // SUBLANES, SUBLANES, LRU_WIDTH), 1)
    groups = LRU_WIDTH // LANES
    for b in range(bb):
        x = x_ref[b].astype(F32)
        xbuf_ref[b, HALO:HALO + tok, :] = x
        y = cb_ref[...] + x * cw_ref[CONV_W - 1:CONV_W, :]
        for j in range(CONV_W - 1):
            shift = CONV_W - 1 - j
            y = y + xbuf_ref[b, HALO - shift:HALO - shift + tok, :] * cw_ref[j:j + 1, :]
        xbuf_ref[b, 0:HALO, :] = x[tok - HALO:tok, :]
        ra, rx = [], []
        for gidx in range(groups):
            yg = y[:, gidx * LANES:(gidx + 1) * LANES].astype(BF16)
            ra.append(_dot(yg, wa_ref[gidx]))
            rx.append(_dot(yg, wx_ref[gidx]))
        r = jax.nn.sigmoid(jnp.concatenate(ra, axis=1) + ba_ref[...])
        ig = jax.nn.sigmoid(jnp.concatenate(rx, axis=1) + bx_ref[...])
        log_a = LRU_C * r * neg_sp
        a = jnp.exp(log_a)
        sq = -jnp.tanh(log_a) * (1.0 + a * a)
        mult = jnp.where(sq > 0.0, sq * lax.rsqrt(sq), 0.0)
        mult = jnp.where(first_token, 1.0, mult)
        u = mult * ig * y
        a = a.reshape(tok // SUBLANES, SUBLANES, LRU_WIDTH)
        u = u.reshape(tok // SUBLANES, SUBLANES, LRU_WIDTH)
        d = 1
        while d < SUBLANES:
            keep = sub_row >= d
            a_prev = jnp.where(keep, pltpu.roll(a, d, 1), 1.0)
            u_prev = jnp.where(keep, pltpu.roll(u, d, 1), 0.0)
            u = a * u_prev + u
            a = a * a_prev
            d *= 2
        h_in = h_ref[b]
        hs = []
        for j in range(tok // SUBLANES):
            hs.append(u[j] + a[j] * h_in)
            h_in = hs[-1][SUBLANES - 1:SUBLANES, :]
        h_ref[b] = h_in
        hcur = jnp.concatenate(hs, axis=0)
        o_ref[b] = (hcur * _silu(g_ref[b].astype(F32))).astype(o_ref.dtype)


def _block_diag_pairs(w):
    n, bw, _ = w.shape
    z = jnp.zeros((n // 2, bw, bw), w.dtype)
    top = jnp.concatenate([w[0::2], z], axis=2)
    bot = jnp.concatenate([z, w[1::2]], axis=2)
    return jnp.concatenate([top, bot], axis=1).astype(BF16)


def _lru(proj, conv_w, conv_b, w_a, b_a, w_x, b_x, lam, *, bb=2):
    bsz, seq, _ = proj.shape
    w = LRU_WIDTH
    first = proj.shape[2] // w - 2
    row = lambda a: a.reshape(1, w).astype(F32)
    blk = lambda c: pl.BlockSpec((bb, LRU_TOK, w), lambda b, i: (b, i, c))
    whole = lambda shape: pl.BlockSpec(shape, lambda b, i: (0,) * len(shape))
    groups = w // LANES
    return pl.pallas_call(
        functools.partial(_lru_kernel, bb=bb),
        grid=(bsz // bb, seq // LRU_TOK),
        in_specs=[blk(first), blk(first + 1), whole((CONV_W, w)), whole((1, w)),
                  whole((groups, LANES, LANES)), whole((1, w)),
                  whole((groups, LANES, LANES)), whole((1, w)), whole((1, w))],
        out_specs=pl.BlockSpec((bb, LRU_TOK, w), lambda b, i: (b, i, 0)),
        out_shape=jax.ShapeDtypeStruct((bsz, seq, w), BF16),
        scratch_shapes=[pltpu.VMEM((bb, HALO + LRU_TOK, w), F32), pltpu.VMEM((bb, 1, w), F32)],
        compiler_params=pltpu.CompilerParams(dimension_semantics=("arbitrary", "arbitrary")),
        name="rg_lru",
    )(proj, proj, conv_w.astype(F32), row(conv_b), _block_diag_pairs(w_a), row(b_a),
      _block_diag_pairs(w_x), row(b_x), row(lam))


def _rotate_half_columns(w, heads, dk):
    d = w.shape[0]
    wh = w.reshape(d, heads, dk)
    half = dk // 2
    return jnp.concatenate([-wh[..., half:], wh[..., :half]], axis=-1).reshape(d, heads * dk)


def _even_in_weights(w_in):
    qk = RET_HEADS * RET_DK
    wq, wk, rest = w_in[:, :qk], w_in[:, qk:2 * qk], w_in[:, 2 * qk:]
    sq = slice(2 * RET_HEADS * RET_DV, 2 * RET_HEADS * RET_DV + SB_HEADS * SB_DH)
    rest = rest.at[:, sq].multiply(SB_DH ** -0.5 * LOG2E)
    cols = [wq, _rotate_half_columns(wq, RET_HEADS, RET_DK),
            wk, _rotate_half_columns(wk, RET_HEADS, RET_DK), rest]
    return jnp.concatenate(cols, axis=1).astype(BF16)


def kernel(x, pre_norm_w, post_norm_w, even_w_in, even_w_out, odd_w_in, odd_w_out, hgrn_lb_logits,
           conv_w, conv_b, lru_w_a, lru_b_a, lru_w_x, lru_b_x, lru_lambda):
    bsz, seq, d = x.shape
    depth = pre_norm_w.shape[0]
    x2d = x.reshape(bsz * seq, d)
    for layer in range(depth):
        idx = layer // 2
        if layer % 2 == 0:
            proj = _in_proj(x2d, pre_norm_w[layer], _even_in_weights(even_w_in[idx]))
            proj = proj.reshape(bsz, seq, -1)
            mix_a = _retention(proj)
            mix_b = _stick_breaking(proj)
            w_out = even_w_out[idx]
        else:
            proj = _in_proj(x2d, pre_norm_w[layer], odd_w_in[idx].astype(BF16))
            proj = proj.reshape(bsz, seq, -1)
            mix_a = _hgrn(proj, hgrn_lb_logits.astype(F32), idx)
            mix_b = _lru(proj, conv_w[idx], conv_b[idx], lru_w_a[idx], lru_b_a[idx],
                         lru_w_x[idx], lru_b_x[idx], lru_lambda[idx])
            w_out = odd_w_out[idx]
        x2d = _out_proj(mix_a.reshape(bsz * seq, -1), mix_b.reshape(bsz * seq, -1),
                        w_out.astype(BF16), post_norm_w[layer], x2d)
    return x2d.reshape(bsz, seq, d)
```

```python
import functools

import numpy as np
import jax
import jax.numpy as jnp
from jax import lax
from jax.experimental import pallas as pl
from jax.experimental.pallas import tpu as pltpu

F32 = jnp.float32
BF16 = jnp.bfloat16

EPS = 1e-6
LANES = 128
SUBLANES = 8
TOK = 128
ROPE_BASE = 10000.0
RET_HEADS, RET_DK, RET_DV = 4, 64, 128
SB_HEADS, SB_DH = 8, 64
HG_HEADS = 4
LRU_WIDTH, LRU_BLOCKS, CONV_W, LRU_C = 512, 8, 4, 8.0
LOG2E = float(np.log2(np.e))
F32_EXP2_UNDERFLOW = -150.0


def _dot(a, b):
    return jnp.dot(a, b, preferred_element_type=F32)


def _dot_nt(a, b):
    return lax.dot_general(a, b, (((1,), (1,)), ((), ())), preferred_element_type=F32)


def _dot_tn(a, b):
    return lax.dot_general(a, b, (((0,), (0,)), ((), ())), preferred_element_type=F32)


def _log_sigmoid(z):
    return jnp.minimum(z, 0.0) - jnp.log(1.0 + jnp.exp(-jnp.abs(z)))


def _silu(g):
    return g * jax.nn.sigmoid(g)


def _in_proj_kernel(x_ref, nw_ref, w_ref, o_ref, *, n_chunk):
    x = x_ref[...]
    ms = jnp.mean(x * x, axis=-1, keepdims=True)
    h = (x * lax.rsqrt(ms + EPS) * nw_ref[...]).astype(BF16)
    for c in range(o_ref.shape[1] // n_chunk):
        cols = slice(c * n_chunk, (c + 1) * n_chunk)
        o_ref[:, cols] = _dot(h, w_ref[:, cols]).astype(o_ref.dtype)


def _in_proj(x2d, norm_w, w_bf16, *, tm=512, n_chunk=512):
    m, d = x2d.shape
    n = w_bf16.shape[1]
    return pl.pallas_call(
        functools.partial(_in_proj_kernel, n_chunk=n_chunk),
        grid=(m // tm,),
        in_specs=[
            pl.BlockSpec((tm, d), lambda i: (i, 0)),
            pl.BlockSpec((1, d), lambda i: (0, 0)),
            pl.BlockSpec((d, n), lambda i: (0, 0)),
        ],
        out_specs=pl.BlockSpec((tm, n), lambda i: (i, 0)),
        out_shape=jax.ShapeDtypeStruct((m, n), BF16),
        compiler_params=pltpu.CompilerParams(dimension_semantics=("arbitrary",)),
        name="in_proj",
    )(x2d, norm_w.reshape(1, d), w_bf16)


def _out_proj_kernel(ma_ref, mb_ref, w_ref, nw_ref, x_ref, o_ref):
    half = ma_ref.shape[1]
    y = _dot(ma_ref[...], w_ref[:half, :]) + _dot(mb_ref[...], w_ref[half:, :])
    ms = jnp.mean(y * y, axis=-1, keepdims=True)
    o_ref[...] = x_ref[...] + y * lax.rsqrt(ms + EPS) * nw_ref[...]


def _out_proj(mix_a, mix_b, w_bf16, norm_w, x2d, *, tm=512):
    m, d = x2d.shape
    half = mix_a.shape[1]
    return pl.pallas_call(
        _out_proj_kernel,
        grid=(m // tm,),
        in_specs=[
            pl.BlockSpec((tm, half), lambda i: (i, 0)),
            pl.BlockSpec((tm, half), lambda i: (i, 0)),
            pl.BlockSpec((2 * half, d), lambda i: (0, 0)),
            pl.BlockSpec((1, d), lambda i: (0, 0)),
            pl.BlockSpec((tm, d), lambda i: (i, 0)),
        ],
        out_specs=pl.BlockSpec((tm, d), lambda i: (i, 0)),
        out_shape=jax.ShapeDtypeStruct((m, d), F32),
        compiler_params=pltpu.CompilerParams(dimension_semantics=("arbitrary",)),
        name="out_proj",
    )(mix_a, mix_b, w_bf16, norm_w.reshape(1, d), x2d)


def _out_in_proj_kernel(ma_ref, mb_ref, wo_ref, pw_ref, x_ref, nw_ref, wi_ref, xo_ref, proj_ref,
                        *, n_chunk, splits):
    half = ma_ref.shape[1]
    rows = x_ref.shape[0] // splits
    parts = [slice(s * rows, (s + 1) * rows) for s in range(splits)]
    ys = [_dot(ma_ref[r, :], wo_ref[:half, :]) + _dot(mb_ref[r, :], wo_ref[half:, :])
          for r in parts]
    hs = []
    for r, y in zip(parts, ys):
        ms = jnp.mean(y * y, axis=-1, keepdims=True)
        xn = x_ref[r, :] + y * lax.rsqrt(ms + EPS) * pw_ref[...]
        xo_ref[r, :] = xn
        ms = jnp.mean(xn * xn, axis=-1, keepdims=True)
        hs.append((xn * lax.rsqrt(ms + EPS) * nw_ref[...]).astype(BF16))
    for r, h in zip(parts, hs):
        for c in range(proj_ref.shape[1] // n_chunk):
            cols = slice(c * n_chunk, (c + 1) * n_chunk)
            proj_ref[r, cols] = _dot(h, wi_ref[:, cols]).astype(proj_ref.dtype)


def _out_in_proj(mix_a, mix_b, w_out_bf16, post_w, x2d, pre_w, w_in_bf16, *, tm=512, n_chunk=512,
                 splits=2):
    m, d = x2d.shape
    half = mix_a.shape[1]
    n = w_in_bf16.shape[1]
    tile = lambda width: pl.BlockSpec((tm, width), lambda i: (i, 0))
    whole = lambda shape: pl.BlockSpec(shape, lambda i: (0,) * len(shape))
    return pl.pallas_call(
        functools.partial(_out_in_proj_kernel, n_chunk=n_chunk, splits=splits),
        grid=(m // tm,),
        in_specs=[tile(half), tile(half), whole((2 * half, d)), whole((1, d)), tile(d),
                  whole((1, d)), whole((d, n))],
        out_specs=[tile(d), tile(n)],
        out_shape=[jax.ShapeDtypeStruct((m, d), F32), jax.ShapeDtypeStruct((m, n), BF16)],
        compiler_params=pltpu.CompilerParams(dimension_semantics=("arbitrary",)),
        name="out_in_proj",
    )(mix_a, mix_b, w_out_bf16, post_w.reshape(1, d), x2d, pre_w.reshape(1, d), w_in_bf16)


def _retention_kernel(q_ref, q2_ref, k_ref, k2_ref, v_ref, g_ref, cos_ref, sin_ref,
                      decay_ref, qd_ref, kd_ref, o_ref, state_ref, *, bb, g_chunk):
    @pl.when(pl.program_id(1) == 0)
    def _():
        state_ref[...] = jnp.zeros_like(state_ref)

    cos = cos_ref[...]
    sin = sin_ref[...]
    lane = lax.broadcasted_iota(jnp.int32, (TOK, LANES), 1)
    head_mask = (lane < RET_DK, lane >= RET_DK)
    units = [(b, h) for b in range(bb) for h in range(RET_HEADS)]
    hl = lambda h: slice(h * LANES, (h + 1) * LANES)
    qm, kp, vs, states = {}, {}, {}, {}
    for b in range(bb):
        qr = q_ref[b].astype(F32) * cos + q2_ref[b].astype(F32) * sin
        kr = (k_ref[b].astype(F32) * cos + k2_ref[b].astype(F32) * sin) * (RET_DK ** -0.5)
        for h in range(RET_HEADS):
            lanes = hl(h // 2)
            qm[b, h] = jnp.where(head_mask[h % 2], qr[:, lanes], 0.0)
            kp[b, h] = kr[:, lanes].astype(BF16)
            vs[b, h] = v_ref[b, :, hl(h)]
            states[b, h] = state_ref[b, h]
    scores = {u: _dot_nt(qm[u].astype(BF16), kp[u]) for u in units}
    inter = {u: _dot((qm[u] * qd_ref[u[1]]).astype(BF16), states[u].astype(BF16)) for u in units}
    kv = {u: _dot_tn(kp[u], (vs[u].astype(F32) * kd_ref[u[1]]).astype(BF16)) for u in units}
    outs = {u: _dot((scores[u] * decay_ref[u[1]]).astype(BF16), vs[u]) + inter[u] for u in units}
    for b, h in units:
        o = outs[b, h]
        state_ref[b, h] = states[b, h] * g_chunk[h] + kv[b, h]
        ms = jnp.mean(o * o, axis=-1, keepdims=True)
        gate = _silu(g_ref[b, :, hl(h)].astype(F32))
        o_ref[b, :, hl(h)] = (o * lax.rsqrt(ms + EPS) * gate).astype(o_ref.dtype)


def _retention_consts(seq):
    h = np.arange(RET_HEADS, dtype=np.float64)
    log_g = np.log(1.0 - 2.0 ** (-5.0 - h))
    pos = np.arange(TOK, dtype=np.float64)
    dist = pos[:, None] - pos[None, :]
    decay = np.where(dist >= 0, np.exp(log_g[:, None, None] * np.maximum(dist, 0.0)), 0.0)
    q_decay = np.exp(log_g[:, None] * (pos + 1.0)[None, :])
    k_decay = np.exp(log_g[:, None] * (TOK - 1.0 - pos)[None, :])
    ones = np.ones((1, 1, LANES))
    g_chunk = tuple(float(v) for v in np.exp(log_g * TOK))
    half = RET_DK // 2
    inv = ROPE_BASE ** (-np.arange(half, dtype=np.float64) / half)
    ang = np.arange(seq, dtype=np.float64)[:, None] * inv[None, :]
    cos = np.tile(np.cos(ang), (1, 2 * RET_HEADS))
    sin = np.tile(np.sin(ang), (1, 2 * RET_HEADS))
    to = lambda a: jnp.asarray(a, dtype=F32)
    return (to(cos), to(sin), to(decay), to(q_decay[:, :, None] * ones),
            to(k_decay[:, :, None] * ones), g_chunk)


def _retention(proj, *, bb=4):
    bsz, seq, _ = proj.shape
    cos, sin, decay, qd, kd, g_chunk = _retention_consts(seq)
    qk_w = RET_HEADS * RET_DK
    v_w = RET_HEADS * RET_DV
    qk_spec = lambda j: pl.BlockSpec((bb, TOK, qk_w), lambda b, c: (b, c, j))
    v_spec = lambda j: pl.BlockSpec((bb, TOK, v_w), lambda b, c: (b, c, j))
    tab_spec = pl.BlockSpec((TOK, qk_w), lambda b, c: (c, 0))
    const_spec = pl.BlockSpec((RET_HEADS, TOK, LANES), lambda b, c: (0, 0, 0))
    return pl.pallas_call(
        functools.partial(_retention_kernel, bb=bb, g_chunk=g_chunk),
        grid=(bsz // bb, seq // TOK),
        in_specs=[qk_spec(0), qk_spec(1), qk_spec(2), qk_spec(3), v_spec(2), v_spec(3),
                  tab_spec, tab_spec, const_spec, const_spec, const_spec],
        out_specs=pl.BlockSpec((bb, TOK, v_w), lambda b, c: (b, c, 0)),
        out_shape=jax.ShapeDtypeStruct((bsz, seq, v_w), BF16),
        scratch_shapes=[pltpu.VMEM((bb, RET_HEADS, LANES, RET_DV), F32)],
        compiler_params=pltpu.CompilerParams(dimension_semantics=("arbitrary", "arbitrary")),
        name="retention",
    )(proj, proj, proj, proj, proj, proj, cos, sin, decay, qd, kd)


SB_NEAR = 3
SB_SUB = 64
SB_KEYS = 256


def _log2_sigmoid_pair(y):
    soft = jnp.log(1.0 + jnp.exp2(-jnp.abs(y))) * LOG2E
    log_beta = jnp.minimum(y, 0.0) - soft
    return log_beta, log_beta - y


def _stick_breaking_kernel(q_ref, k_ref, v_ref, g_ref, wtri_ref, tri_ref, o_ref,
                           acc_ref, total_ref, carry_ref, *, bb):
    i = pl.program_id(1)
    n_pairs = SB_HEADS // 2
    win = (SB_NEAR - 1) * TOK
    pair_lanes = [slice(p * LANES, (p + 1) * LANES) for p in range(n_pairs)]
    units = [(b, p) for b in range(bb) for p in range(n_pairs)]

    def pick_head(from_first, from_second):
        first = lax.broadcasted_iota(jnp.int32, from_first.shape, 1) < SB_DH
        return jnp.where(first, from_first, from_second)

    def masked_heads(x):
        first = lax.broadcasted_iota(jnp.int32, x.shape, 1) < SB_DH
        zero = jnp.zeros_like(x)
        return jnp.where(first, x, zero), jnp.where(first, zero, x)

    def write_out():
        for b, p in units:
            gate = _silu(g_ref[b, :, pair_lanes[p]].astype(F32))
            o_ref[b, :, pair_lanes[p]] = (acc_ref[b, p] * gate).astype(o_ref.dtype)

    def finish(accs, totals):
        gates = {(b, p): g_ref[b, :, pair_lanes[p]] for b, p in units}
        for b, p in units:
            acc_ref[b, p] = accs[b, p]
            total_ref[b, p] = totals[b, p]
            o_ref[b, :, pair_lanes[p]] = (
                accs[b, p] * _silu(gates[b, p].astype(F32))).astype(o_ref.dtype)
        return jnp.max(functools.reduce(jnp.maximum, totals.values()))


    def near_first_blocks():
        row = lax.broadcasted_iota(jnp.int32, (2 * TOK, win), 0) & (TOK - 1)
        col = lax.broadcasted_iota(jnp.int32, (2 * TOK, win), 1)
        strict = (col - row) < i * TOK
        qs = {(b, p): q_ref[b, :, pair_lanes[p]] for b, p in units}
        ks = {(b, p): k_ref[b, 0:win, pair_lanes[p]] for b, p in units}
        vs = {(b, p): v_ref[b, 0:win, pair_lanes[p]] for b, p in units}
        wtri = wtri_ref[0:win, 0:win]
        ys = {u: _dot_nt(jnp.concatenate(masked_heads(qs[u]), axis=0), ks[u]) for u in units}
        log_betas, suffixes, totals, accs = {}, {}, {}, {}
        for u in units:
            log_betas[u], log_rest = _log2_sigmoid_pair(ys[u])
            log_rest = jnp.where(strict, log_rest, 0.0).astype(BF16)
            suffixes[u] = _dot(log_rest, wtri)
            totals[u] = suffixes[u][:, 0:1] + log_rest[:, 0:1].astype(F32)
        for u in units:
            w = jnp.where(strict, jnp.exp2(log_betas[u] + suffixes[u]), 0.0).astype(BF16)
            accs[u] = pick_head(_dot(w[:TOK], vs[u]), _dot(w[TOK:], vs[u]))
        return finish(accs, totals)

    def near():
        subs = TOK // SB_SUB
        rho = lax.broadcasted_iota(jnp.int32, (2 * SB_SUB, LANES), 0) & (SB_SUB - 1)
        col = lax.broadcasted_iota(jnp.int32, (2 * SB_SUB, LANES), 1)
        strict = col < rho + (LANES - SB_SUB)

        def mask(x):
            body = SB_KEYS - LANES
            return jnp.concatenate([x[:, :body], jnp.where(strict, x[:, body:], 0.0)], axis=1)

        tiles = [(b, p, s) for b, p in units for s in range(subs)]
        tri = wtri_ref[0:SB_KEYS, 0:SB_KEYS]
        lhs, ks, vs = {}, {}, {}
        for b, p in units:
            q0, q1 = masked_heads(q_ref[b, :, pair_lanes[p]])
            for s in range(subs):
                sub = slice(s * SB_SUB, (s + 1) * SB_SUB)
                start = pl.multiple_of((i * TOK + (s + 1) * SB_SUB) - SB_KEYS, SB_SUB)
                lhs[b, p, s] = jnp.concatenate([q0[sub], q1[sub]], axis=0)
                ks[b, p, s] = k_ref[b, pl.ds(start, SB_KEYS), pair_lanes[p]]
                vs[b, p, s] = v_ref[b, pl.ds(start, SB_KEYS), pair_lanes[p]]
        ys = {t: _dot_nt(lhs[t], ks[t]) for t in tiles}
        log_betas, suffixes, sub_totals, sub_accs = {}, {}, {}, {}
        for t in tiles:
            log_betas[t], log_rest = _log2_sigmoid_pair(ys[t])
            log_rest = mask(log_rest).astype(BF16)
            suffixes[t] = _dot(log_rest, tri)
            sub_totals[t] = suffixes[t][:, 0:1] + log_rest[:, 0:1].astype(F32)
        for t in tiles:
            w = mask(jnp.exp2(log_betas[t] + suffixes[t])).astype(BF16)
            sub_accs[t] = pick_head(_dot(w[:SB_SUB], vs[t]), _dot(w[SB_SUB:], vs[t]))
        accs = {(b, p): jnp.concatenate([sub_accs[b, p, s] for s in range(subs)], axis=0)
                for b, p in units}
        totals = {(b, p): jnp.concatenate(
            [sub_totals[b, p, s][hh * SB_SUB:(hh + 1) * SB_SUB]
             for hh in range(2) for s in range(subs)], axis=0) for b, p in units}
        return finish(accs, totals)

    @pl.when(i < SB_NEAR - 1)
    def _():
        near_first_blocks()

    @pl.when(i >= SB_NEAR - 1)
    def _():
        top = near()

        @pl.when(top >= F32_EXP2_UNDERFLOW)
        def _():
            tri = tri_ref[...]
            for b, p in units:
                for hh in range(2):
                    carry_ref[b, 2 * p + hh] = jnp.broadcast_to(
                        total_ref[b, p, hh * TOK:(hh + 1) * TOK, :], (TOK, TOK))
            row = lax.broadcasted_iota(jnp.int32, (TOK, TOK), 0)
            col = lax.broadcasted_iota(jnp.int32, (TOK, TOK), 1)
            uncovered = col < (row // SB_SUB + 1) * SB_SUB + (SB_NEAR - 1) * TOK - SB_KEYS

            def visit(j, partly_covered):
                blk = pl.ds(pl.multiple_of(j * TOK, TOK), TOK)
                for b, p in units:
                    qs = masked_heads(q_ref[b, :, pair_lanes[p]])
                    vs = masked_heads(v_ref[b, blk, pair_lanes[p]])
                    kp = k_ref[b, blk, pair_lanes[p]]
                    for hh in range(2):
                        h = 2 * p + hh
                        log_beta, log_rest = _log2_sigmoid_pair(_dot_nt(qs[hh], kp))
                        if partly_covered:
                            log_rest = jnp.where(uncovered, log_rest, 0.0)
                        sums = _dot(log_rest.astype(BF16), tri)
                        carry = carry_ref[b, h]
                        w = jnp.exp2(log_beta + sums[:, :TOK] + carry)
                        if partly_covered:
                            w = jnp.where(uncovered, w, 0.0)
                        acc_ref[b, p] += _dot(w.astype(BF16), vs[hh])
                        carry_ref[b, h] = carry + sums[:, TOK:]

            visit(i - (SB_NEAR - 1), True)

            def cond(state):
                d, far_top = state
                return jnp.logical_and(d <= i, far_top >= F32_EXP2_UNDERFLOW)

            def body(state):
                d, _ = state
                visit(i - d, False)
                return d + 1, jnp.max(carry_ref[...])

            lax.while_loop(cond, body, (jnp.int32(SB_NEAR), jnp.max(carry_ref[...])))
            write_out()


def _stick_breaking(proj, *, bb=2):
    bsz, seq, _ = proj.shape
    w = SB_HEADS * SB_DH
    first = proj.shape[2] // w - 4
    win = max(SB_KEYS, (SB_NEAR - 1) * TOK)
    j = np.arange(win)
    later = (j[:, None] > j[None, :]).astype(np.float32)
    tri = np.concatenate([later[:TOK, :TOK], np.ones((TOK, TOK), np.float32)], axis=1)
    blk = lambda c: pl.BlockSpec((bb, TOK, w), lambda b, i: (b, i, c))
    full = lambda c: pl.BlockSpec((bb, seq, w), lambda b, i: (b, 0, c))
    whole = lambda shape: pl.BlockSpec(shape, lambda b, i: (0,) * len(shape))
    return pl.pallas_call(
        functools.partial(_stick_breaking_kernel, bb=bb),
        grid=(bsz // bb, seq // TOK),
        in_specs=[blk(first), full(first + 1), full(first + 2), blk(first + 3),
                  whole((win, win)), whole((TOK, 2 * TOK))],
        out_specs=pl.BlockSpec((bb, TOK, w), lambda b, i: (b, i, 0)),
        out_shape=jax.ShapeDtypeStruct((bsz, seq, w), BF16),
        scratch_shapes=[pltpu.VMEM((bb, SB_HEADS // 2, TOK, LANES), F32),
                        pltpu.VMEM((bb, SB_HEADS // 2, 2 * TOK, 1), F32),
                        pltpu.VMEM((bb, SB_HEADS, TOK, TOK), F32)],
        compiler_params=pltpu.CompilerParams(dimension_semantics=("arbitrary", "arbitrary")),
        name="stick_breaking",
    )(proj, proj, proj, proj, jnp.asarray(later, dtype=BF16), jnp.asarray(tri, dtype=BF16))


HG_LEVELS = tuple(2 ** e for e in range(int(np.log2(TOK))))
HG_FIRST_VPU_LEVEL = SUBLANES // 2


def _half_boundary(cum3, last_rows, m):
    if 2 * m == SUBLANES:
        return jnp.broadcast_to(cum3[:, m - 1:m, :], cum3.shape)
    per_block = 2 * m // SUBLANES
    picks = [(g // per_block) * per_block + per_block // 2 - 1 for g in range(cum3.shape[0])]
    return jnp.concatenate([last_rows[g:g + 1] for g in picks], axis=0)


def _hgrn_consts():
    t = np.arange(TOK)
    tt, uu = t[:, None], t[None, :]
    mats = [(uu <= tt)]
    pair_masks = [np.eye(TOK, dtype=bool)]
    for m in HG_LEVELS:
        same_block = (tt // (2 * m)) == (uu // (2 * m))
        up_t = (tt % (2 * m)) >= m
        up_u = (uu % (2 * m)) >= m
        if m < HG_FIRST_VPU_LEVEL:
            mats.append(same_block & (up_t == up_u) & np.where(up_t, uu <= tt, uu > tt))
        pair_masks.append(same_block & up_t & ~up_u)
    prefix = np.concatenate(mats, axis=0).astype(np.float32)
    return jnp.asarray(prefix, dtype=BF16), jnp.asarray(np.stack(pair_masks), dtype=F32)


def _hgrn_kernel(q_ref, f_ref, i_ref, g_ref, lbl_ref, prefix_ref, pm_ref, o_ref, state_ref,
                 *, bb, layer):
    @pl.when(pl.program_id(1) == 0)
    def _():
        state_ref[...] = jnp.zeros_like(state_ref)

    logits = lbl_ref[...]
    e = jnp.exp(logits - jnp.max(logits, axis=0, keepdims=True))
    soft = e / jnp.sum(e, axis=0, keepdims=True)
    lb_all = jnp.zeros_like(soft[0:1])
    for r in range(1, layer + 1):
        lb_all = lb_all + soft[r:r + 1]
    x1 = jnp.log(lb_all)
    log_keep = jnp.log(1.0 - lb_all)
    prefix = prefix_ref[...]
    n_lv = len(HG_LEVELS)
    units = [(b, h) for b in range(bb) for h in range(HG_HEADS)]
    hl = lambda h: slice(h * LANES, (h + 1) * LANES)
    qs, kks, vs, states, cums, rests, level_sums = {}, {}, {}, {}, {}, {}, {}
    for b in range(bb):
        for p in range(HG_HEADS // 2):
            lanes = slice(2 * p * LANES, 2 * (p + 1) * LANES)
            fl = f_ref[b, :, lanes].astype(F32)
            ls = _log_sigmoid(fl)
            x2 = log_keep[:, lanes] + ls
            d = x1[:, lanes] - x2
            log_f = jnp.maximum(x1[:, lanes], x2) + jnp.log(1.0 + jnp.exp(jnp.minimum(d, -d)))
            pair_sums = _dot(prefix, (log_f * LOG2E).astype(BF16))
            cum = pair_sums[0:TOK]
            levels = [pair_sums[(1 + lv) * TOK:(2 + lv) * TOK]
                      for lv in range(n_lv) if HG_LEVELS[lv] < HG_FIRST_VPU_LEVEL]
            cum3 = cum.reshape(TOK // SUBLANES, SUBLANES, 2 * LANES)
            last_rows = jnp.broadcast_to(cum3[:, SUBLANES - 1:SUBLANES, :], cum3.shape)
            for m in HG_LEVELS[len(levels):]:
                gap = cum3 - _half_boundary(cum3, last_rows, m)
                levels.append(jnp.minimum(gap, -gap).reshape(TOK, 2 * LANES))
            rest = cum[TOK - 1:TOK, :] - cum
            kk = (1.0 - lb_all[:, lanes]) * jnp.exp(ls - fl)
            for hh in range(2):
                u = (b, 2 * p + hh)
                head = slice(hh * LANES, (hh + 1) * LANES)
                cums[u], rests[u], kks[u] = cum[:, head], rest[:, head], kk[:, head]
                level_sums[u] = [x[:, head] for x in levels]
    for b, h in units:
        qs[b, h] = q_ref[b, :, hl(h)].astype(F32)
        vs[b, h] = i_ref[b, :, hl(h)]
        states[b, h] = state_ref[b, h]
    attns = {}
    for u in units:
        q, kk = qs[u], kks[u]
        attn = _dot_nt(q.astype(BF16), kk.astype(BF16)) * pm_ref[0]
        for lv in range(n_lv):
            dec = jnp.exp2(level_sums[u][lv])
            attn = attn + _dot_nt((q * dec).astype(BF16), (kk * dec).astype(BF16)) * pm_ref[lv + 1]
        attns[u] = attn.astype(BF16)
    outs, kvs = {}, {}
    for u in units:
        ktail = (kks[u] * jnp.exp2(rests[u])).astype(BF16)
        q_in = (qs[u] * jnp.exp2(cums[u])).astype(BF16)
        outs[u] = _dot(attns[u], vs[u]) + _dot_nt(q_in, states[u].astype(BF16))
        kvs[u] = _dot_tn(vs[u], ktail)
    for b, h in units:
        o = outs[b, h]
        state_ref[b, h] = states[b, h] * jnp.exp2(cums[b, h][TOK - 1:TOK, :]) + kvs[b, h]
        ms = jnp.mean(o * o, axis=-1, keepdims=True)
        gate = _silu(g_ref[b, :, hl(h)].astype(F32))
        o_ref[b, :, hl(h)] = (o * lax.rsqrt(ms + EPS) * gate).astype(o_ref.dtype)


def _hgrn(proj, lb_logits, layer, *, bb=4):
    bsz, seq, _ = proj.shape
    w = HG_HEADS * LANES
    prefix, pair_masks = _hgrn_consts()
    blk = lambda c: pl.BlockSpec((bb, TOK, w), lambda b, i: (b, i, c))
    whole = lambda a: pl.BlockSpec(a.shape, lambda b, i: (0,) * a.ndim)
    return pl.pallas_call(
        functools.partial(_hgrn_kernel, bb=bb, layer=layer),
        grid=(bsz // bb, seq // TOK),
        in_specs=[blk(0), blk(1), blk(2), blk(3), whole(lb_logits), whole(prefix),
                  whole(pair_masks)],
        out_specs=pl.BlockSpec((bb, TOK, w), lambda b, i: (b, i, 0)),
        out_shape=jax.ShapeDtypeStruct((bsz, seq, w), BF16),
        scratch_shapes=[pltpu.VMEM((bb, HG_HEADS, LANES, LANES), F32)],
        compiler_params=pltpu.CompilerParams(dimension_semantics=("arbitrary", "arbitrary")),
        name="hgrn2",
    )(proj, proj, proj, proj, lb_logits, prefix, pair_masks)


LRU_TOK = 256
HALO = 8


def _lru_kernel(x_ref, g_ref, cw_ref, cb_ref, wa_ref, ba_ref, wx_ref, bx_ref, lam_ref, o_ref,
                xbuf_ref, h_ref, *, bb):
    c = pl.program_id(1)
    tok = x_ref.shape[1]

    @pl.when(c == 0)
    def _():
        xbuf_ref[...] = jnp.zeros_like(xbuf_ref)
        h_ref[...] = jnp.zeros_like(h_ref)

    lam = lam_ref[...]
    neg_sp = -(jnp.maximum(-lam, 0.0) + jnp.log(1.0 + jnp.exp(-jnp.abs(lam))))
    row = lax.broadcasted_iota(jnp.int32, (tok, LRU_WIDTH), 0)
    first_token = jnp.logical_and(row == 0, c == 0)
    sub_row = lax.broadcasted_iota(jnp.int32, (tok // SUBLANES, SUBLANES, LRU_WIDTH), 1)
    groups = LRU_WIDTH // LANES
    for b in range(bb):
        x = x_ref[b].astype(F32)
        xbuf_ref[b, HALO:HALO + tok, :] = x
        y = cb_ref[...] + x * cw_ref[CONV_W - 1:CONV_W, :]
        for j in range(CONV_W - 1):
            shift = CONV_W - 1 - j
            y = y + xbuf_ref[b, HALO - shift:HALO - shift + tok, :] * cw_ref[j:j + 1, :]
        xbuf_ref[b, 0:HALO, :] = x[tok - HALO:tok, :]
        ra, rx = [], []
        for gidx in range(groups):
            yg = y[:, gidx * LANES:(gidx + 1) * LANES].astype(BF16)
            ra.append(_dot(yg, wa_ref[gidx]))
            rx.append(_dot(yg, wx_ref[gidx]))
        r = jax.nn.sigmoid(jnp.concatenate(ra, axis=1) + ba_ref[...])
        ig = jax.nn.sigmoid(jnp.concatenate(rx, axis=1) + bx_ref[...])
        log_a = LRU_C * r * neg_sp
        a = jnp.exp(log_a)
        sq = -jnp.tanh(log_a) * (1.0 + a * a)
        mult = jnp.where(sq > 0.0, sq * lax.rsqrt(sq), 0.0)
        mult = jnp.where(first_token, 1.0, mult)
        u = mult * ig * y
        a = a.reshape(tok // SUBLANES, SUBLANES, LRU_WIDTH)
        u = u.reshape(tok // SUBLANES, SUBLANES, LRU_WIDTH)
        d = 1
        while d < SUBLANES:
            keep = sub_row >= d
            a_prev = jnp.where(keep, pltpu.roll(a, d, 1), 1.0)
            u_prev = jnp.where(keep, pltpu.roll(u, d, 1), 0.0)
            u = a * u_prev + u
            a = a * a_prev
            d *= 2
        h_in = h_ref[b]
        hs = []
        for j in range(tok // SUBLANES):
            hs.append(u[j] + a[j] * h_in)
            h_in = hs[-1][SUBLANES - 1:SUBLANES, :]
        h_ref[b] = h_in
        hcur = jnp.concatenate(hs, axis=0)
        o_ref[b] = (hcur * _silu(g_ref[b].astype(F32))).astype(o_ref.dtype)


def _block_diag_pairs(w):
    n, bw, _ = w.shape
    z = jnp.zeros((n // 2, bw, bw), w.dtype)
    top = jnp.concatenate([w[0::2], z], axis=2)
    bot = jnp.concatenate([z, w[1::2]], axis=2)
    return jnp.concatenate([top, bot], axis=1).astype(BF16)


def _lru(proj, conv_w, conv_b, w_a, b_a, w_x, b_x, lam, *, bb=2):
    bsz, seq, _ = proj.shape
    w = LRU_WIDTH
    first = proj.shape[2] // w - 2
    row = lambda a: a.reshape(1, w).astype(F32)
    blk = lambda c: pl.BlockSpec((bb, LRU_TOK, w), lambda b, i: (b, i, c))
    whole = lambda shape: pl.BlockSpec(shape, lambda b, i: (0,) * len(shape))
    groups = w // LANES
    return pl.pallas_call(
        functools.partial(_lru_kernel, bb=bb),
        grid=(bsz // bb, seq // LRU_TOK),
        in_specs=[blk(first), blk(first + 1), whole((CONV_W, w)), whole((1, w)),
                  whole((groups, LANES, LANES)), whole((1, w)),
                  whole((groups, LANES, LANES)), whole((1, w)), whole((1, w))],
        out_specs=pl.BlockSpec((bb, LRU_TOK, w), lambda b, i: (b, i, 0)),
        out_shape=jax.ShapeDtypeStruct((bsz, seq, w), BF16),
        scratch_shapes=[pltpu.VMEM((bb, HALO + LRU_TOK, w), F32), pltpu.VMEM((bb, 1, w), F32)],
        compiler_params=pltpu.CompilerParams(dimension_semantics=("arbitrary", "arbitrary")),
        name="rg_lru",
    )(proj, proj, conv_w.astype(F32), row(conv_b), _block_diag_pairs(w_a), row(b_a),
      _block_diag_pairs(w_x), row(b_x), row(lam))


def _rotate_half_columns(w, heads, dk):
    d = w.shape[0]
    wh = w.reshape(d, heads, dk)
    half = dk // 2
    return jnp.concatenate([-wh[..., half:], wh[..., :half]], axis=-1).reshape(d, heads * dk)


def _even_in_weights(w_in):
    qk = RET_HEADS * RET_DK
    wq, wk, rest = w_in[:, :qk], w_in[:, qk:2 * qk], w_in[:, 2 * qk:]
    sq = slice(2 * RET_HEADS * RET_DV, 2 * RET_HEADS * RET_DV + SB_HEADS * SB_DH)
    rest = rest.at[:, sq].multiply(SB_DH ** -0.5 * LOG2E)
    cols = [wq, _rotate_half_columns(wq, RET_HEADS, RET_DK),
            wk, _rotate_half_columns(wk, RET_HEADS, RET_DK), rest]
    return jnp.concatenate(cols, axis=1).astype(BF16)


def kernel(x, pre_norm_w, post_norm_w, even_w_in, even_w_out, odd_w_in, odd_w_out, hgrn_lb_logits,
           conv_w, conv_b, lru_w_a, lru_b_a, lru_w_x, lru_b_x, lru_lambda):
    bsz, seq, d = x.shape
    depth = pre_norm_w.shape[0]
    x2d = x.reshape(bsz * seq, d)

    def in_weights(layer):
        if layer % 2 == 0:
            return _even_in_weights(even_w_in[layer // 2])
        return odd_w_in[layer // 2].astype(BF16)

    proj = _in_proj(x2d, pre_norm_w[0], in_weights(0))
    for layer in range(depth):
        idx = layer // 2
        proj = proj.reshape(bsz, seq, -1)
        if layer % 2 == 0:
            mix_a = _retention(proj)
            mix_b = _stick_breaking(proj)
            w_out = even_w_out[idx]
        else:
            mix_a = _hgrn(proj, hgrn_lb_logits.astype(F32), idx)
            mix_b = _lru(proj, conv_w[idx], conv_b[idx], lru_w_a[idx], lru_b_a[idx],
                         lru_w_x[idx], lru_b_x[idx], lru_lambda[idx])
            w_out = odd_w_out[idx]
        mix_a = mix_a.reshape(bsz * seq, -1)
        mix_b = mix_b.reshape(bsz * seq, -1)
        if layer + 1 < depth:
            x2d, proj = _out_in_proj(mix_a, mix_b, w_out.astype(BF16), post_norm_w[layer], x2d,
                                     pre_norm_w[layer + 1], in_weights(layer + 1))
        else:
            x2d = _out_proj(mix_a, mix_b, w_out.astype(BF16), post_norm_w[layer], x2d)
    return x2d.reshape(bsz, seq, d)
```

```python
import functools

import numpy as np
import jax
import jax.numpy as jnp
from jax import lax
from jax.experimental import pallas as pl
from jax.experimental.pallas import tpu as pltpu

F32 = jnp.float32
BF16 = jnp.bfloat16

EPS = 1e-6
LANES = 128
SUBLANES = 8
TOK = 128
ROPE_BASE = 10000.0
RET_HEADS, RET_DK, RET_DV = 4, 64, 128
SB_HEADS, SB_DH = 8, 64
HG_HEADS = 4
LRU_WIDTH, LRU_BLOCKS, CONV_W, LRU_C = 512, 8, 4, 8.0
LOG2E = float(np.log2(np.e))
F32_EXP2_UNDERFLOW = -150.0


def _dot(a, b):
    return jnp.dot(a, b, preferred_element_type=F32)


def _dot_nt(a, b):
    return lax.dot_general(a, b, (((1,), (1,)), ((), ())), preferred_element_type=F32)


def _dot_tn(a, b):
    return lax.dot_general(a, b, (((0,), (0,)), ((), ())), preferred_element_type=F32)


def _log_sigmoid(z):
    return jnp.minimum(z, 0.0) - jnp.log(1.0 + jnp.exp(-jnp.abs(z)))


def _silu(g):
    return g * jax.nn.sigmoid(g)


def _in_proj_kernel(x_ref, nw_ref, w_ref, o_ref, *, n_chunk):
    x = x_ref[...]
    ms = jnp.mean(x * x, axis=-1, keepdims=True)
    h = (x * lax.rsqrt(ms + EPS) * nw_ref[...]).astype(BF16)
    for c in range(o_ref.shape[1] // n_chunk):
        cols = slice(c * n_chunk, (c + 1) * n_chunk)
        o_ref[:, cols] = _dot(h, w_ref[:, cols]).astype(o_ref.dtype)


def _in_proj(x2d, norm_w, w_bf16, *, tm=512, n_chunk=512):
    m, d = x2d.shape
    n = w_bf16.shape[1]
    return pl.pallas_call(
        functools.partial(_in_proj_kernel, n_chunk=n_chunk),
        grid=(m // tm,),
        in_specs=[
            pl.BlockSpec((tm, d), lambda i: (i, 0)),
            pl.BlockSpec((1, d), lambda i: (0, 0)),
            pl.BlockSpec((d, n), lambda i: (0, 0)),
        ],
        out_specs=pl.BlockSpec((tm, n), lambda i: (i, 0)),
        out_shape=jax.ShapeDtypeStruct((m, n), BF16),
        compiler_params=pltpu.CompilerParams(dimension_semantics=("arbitrary",)),
        name="in_proj",
    )(x2d, norm_w.reshape(1, d), w_bf16)


def _out_proj_kernel(ma_ref, mb_ref, w_ref, nw_ref, x_ref, o_ref):
    half = ma_ref.shape[1]
    y = _dot(ma_ref[...], w_ref[:half, :]) + _dot(mb_ref[...], w_ref[half:, :])
    ms = jnp.mean(y * y, axis=-1, keepdims=True)
    o_ref[...] = x_ref[...] + y * lax.rsqrt(ms + EPS) * nw_ref[...]


def _out_proj(mix_a, mix_b, w_bf16, norm_w, x2d, *, tm=512):
    m, d = x2d.shape
    half = mix_a.shape[1]
    return pl.pallas_call(
        _out_proj_kernel,
        grid=(m // tm,),
        in_specs=[
            pl.BlockSpec((tm, half), lambda i: (i, 0)),
            pl.BlockSpec((tm, half), lambda i: (i, 0)),
            pl.BlockSpec((2 * half, d), lambda i: (0, 0)),
            pl.BlockSpec((1, d), lambda i: (0, 0)),
            pl.BlockSpec((tm, d), lambda i: (i, 0)),
        ],
        out_specs=pl.BlockSpec((tm, d), lambda i: (i, 0)),
        out_shape=jax.ShapeDtypeStruct((m, d), F32),
        compiler_params=pltpu.CompilerParams(dimension_semantics=("arbitrary",)),
        name="out_proj",
    )(mix_a, mix_b, w_bf16, norm_w.reshape(1, d), x2d)


def _out_in_proj_kernel(ma_ref, mb_ref, wo_ref, pw_ref, x_ref, nw_ref, wi_ref, xo_ref, proj_ref,
                        *, n_chunk, splits):
    half = ma_ref.shape[1]
    rows = x_ref.shape[0] // splits
    parts = [slice(s * rows, (s + 1) * rows) for s in range(splits)]
    ys = [_dot(ma_ref[r, :], wo_ref[:half, :]) + _dot(mb_ref[r, :], wo_ref[half:, :])
          for r in parts]
    hs = []
    for r, y in zip(parts, ys):
        ms = jnp.mean(y * y, axis=-1, keepdims=True)
        xn = x_ref[r, :] + y * lax.rsqrt(ms + EPS) * pw_ref[...]
        xo_ref[r, :] = xn
        ms = jnp.mean(xn * xn, axis=-1, keepdims=True)
        hs.append((xn * lax.rsqrt(ms + EPS) * nw_ref[...]).astype(BF16))
    for r, h in zip(parts, hs):
        for c in range(proj_ref.shape[1] // n_chunk):
            cols = slice(c * n_chunk, (c + 1) * n_chunk)
            proj_ref[r, cols] = _dot(h, wi_ref[:, cols]).astype(proj_ref.dtype)


def _out_in_proj(mix_a, mix_b, w_out_bf16, post_w, x2d, pre_w, w_in_bf16, *, tm=512, n_chunk=512,
                 splits=2):
    m, d = x2d.shape
    half = mix_a.shape[1]
    n = w_in_bf16.shape[1]
    tile = lambda width: pl.BlockSpec((tm, width), lambda i: (i, 0))
    whole = lambda shape: pl.BlockSpec(shape, lambda i: (0,) * len(shape))
    return pl.pallas_call(
        functools.partial(_out_in_proj_kernel, n_chunk=n_chunk, splits=splits),
        grid=(m // tm,),
        in_specs=[tile(half), tile(half), whole((2 * half, d)), whole((1, d)), tile(d),
                  whole((1, d)), whole((d, n))],
        out_specs=[tile(d), tile(n)],
        out_shape=[jax.ShapeDtypeStruct((m, d), F32), jax.ShapeDtypeStruct((m, n), BF16)],
        compiler_params=pltpu.CompilerParams(dimension_semantics=("arbitrary",)),
        name="out_in_proj",
    )(mix_a, mix_b, w_out_bf16, post_w.reshape(1, d), x2d, pre_w.reshape(1, d), w_in_bf16)


def _retention_kernel(q_ref, q2_ref, k_ref, k2_ref, v_ref, g_ref, cos_ref, sin_ref,
                      decay_ref, qd_ref, kd_ref, o_ref, state_ref, *, bb, g_chunk):
    @pl.when(pl.program_id(1) == 0)
    def _():
        state_ref[...] = jnp.zeros_like(state_ref)

    cos = cos_ref[...]
    sin = sin_ref[...]
    lane = lax.broadcasted_iota(jnp.int32, (TOK, LANES), 1)
    head_mask = (lane < RET_DK, lane >= RET_DK)
    units = [(b, h) for b in range(bb) for h in range(RET_HEADS)]
    hl = lambda h: slice(h * LANES, (h + 1) * LANES)
    qm, kp, vs, states = {}, {}, {}, {}
    for b in range(bb):
        qr = q_ref[b].astype(F32) * cos + q2_ref[b].astype(F32) * sin
        kr = (k_ref[b].astype(F32) * cos + k2_ref[b].astype(F32) * sin) * (RET_DK ** -0.5)
        for h in range(RET_HEADS):
            lanes = hl(h // 2)
            qm[b, h] = jnp.where(head_mask[h % 2], qr[:, lanes], 0.0)
            kp[b, h] = kr[:, lanes].astype(BF16)
            vs[b, h] = v_ref[b, :, hl(h)]
            states[b, h] = state_ref[b, h]
    scores = {u: _dot_nt(qm[u].astype(BF16), kp[u]) for u in units}
    inter = {u: _dot((qm[u] * qd_ref[u[1]]).astype(BF16), states[u].astype(BF16)) for u in units}
    kv = {u: _dot_tn(kp[u], (vs[u].astype(F32) * kd_ref[u[1]]).astype(BF16)) for u in units}
    outs = {u: _dot((scores[u] * decay_ref[u[1]]).astype(BF16), vs[u]) + inter[u] for u in units}
    for b, h in units:
        o = outs[b, h]
        state_ref[b, h] = states[b, h] * g_chunk[h] + kv[b, h]
        ms = jnp.mean(o * o, axis=-1, keepdims=True)
        gate = _silu(g_ref[b, :, hl(h)].astype(F32))
        o_ref[b, :, hl(h)] = (o * lax.rsqrt(ms + EPS) * gate).astype(o_ref.dtype)


def _retention_consts(seq):
    h = np.arange(RET_HEADS, dtype=np.float64)
    log_g = np.log(1.0 - 2.0 ** (-5.0 - h))
    pos = np.arange(TOK, dtype=np.float64)
    dist = pos[:, None] - pos[None, :]
    decay = np.where(dist >= 0, np.exp(log_g[:, None, None] * np.maximum(dist, 0.0)), 0.0)
    q_decay = np.exp(log_g[:, None] * (pos + 1.0)[None, :])
    k_decay = np.exp(log_g[:, None] * (TOK - 1.0 - pos)[None, :])
    ones = np.ones((1, 1, LANES))
    g_chunk = tuple(float(v) for v in np.exp(log_g * TOK))
    half = RET_DK // 2
    inv = ROPE_BASE ** (-np.arange(half, dtype=np.float64) / half)
    ang = np.arange(seq, dtype=np.float64)[:, None] * inv[None, :]
    cos = np.tile(np.cos(ang), (1, 2 * RET_HEADS))
    sin = np.tile(np.sin(ang), (1, 2 * RET_HEADS))
    to = lambda a: jnp.asarray(a, dtype=F32)
    return (to(cos), to(sin), to(decay), to(q_decay[:, :, None] * ones),
            to(k_decay[:, :, None] * ones), g_chunk)


def _retention(proj, *, bb=8):
    bsz, seq, _ = proj.shape
    cos, sin, decay, qd, kd, g_chunk = _retention_consts(seq)
    qk_w = RET_HEADS * RET_DK
    v_w = RET_HEADS * RET_DV
    qk_spec = lambda j: pl.BlockSpec((bb, TOK, qk_w), lambda b, c: (b, c, j))
    v_spec = lambda j: pl.BlockSpec((bb, TOK, v_w), lambda b, c: (b, c, j))
    tab_spec = pl.BlockSpec((TOK, qk_w), lambda b, c: (c, 0))
    const_spec = pl.BlockSpec((RET_HEADS, TOK, LANES), lambda b, c: (0, 0, 0))
    return pl.pallas_call(
        functools.partial(_retention_kernel, bb=bb, g_chunk=g_chunk),
        grid=(bsz // bb, seq // TOK),
        in_specs=[qk_spec(0), qk_spec(1), qk_spec(2), qk_spec(3), v_spec(2), v_spec(3),
                  tab_spec, tab_spec, const_spec, const_spec, const_spec],
        out_specs=pl.BlockSpec((bb, TOK, v_w), lambda b, c: (b, c, 0)),
        out_shape=jax.ShapeDtypeStruct((bsz, seq, v_w), BF16),
        scratch_shapes=[pltpu.VMEM((bb, RET_HEADS, LANES, RET_DV), F32)],
        compiler_params=pltpu.CompilerParams(dimension_semantics=("arbitrary", "arbitrary")),
        name="retention",
    )(proj, proj, proj, proj, proj, proj, cos, sin, decay, qd, kd)


SB_NEAR = 3
SB_SUB = 64
SB_KEYS = 256


def _log2_sigmoid_pair(y):
    soft = jnp.log(1.0 + jnp.exp2(-jnp.abs(y))) * LOG2E
    log_beta = jnp.minimum(y, 0.0) - soft
    return log_beta, log_beta - y


def _stick_breaking_kernel(q_ref, k_ref, v_ref, g_ref, wtri_ref, tri_ref, o_ref,
                           acc_ref, total_ref, carry_ref, *, bb):
    i = pl.program_id(1)
    n_pairs = SB_HEADS // 2
    win = (SB_NEAR - 1) * TOK
    pair_lanes = [slice(p * LANES, (p + 1) * LANES) for p in range(n_pairs)]
    units = [(b, p) for b in range(bb) for p in range(n_pairs)]

    def pick_head(from_first, from_second):
        first = lax.broadcasted_iota(jnp.int32, from_first.shape, 1) < SB_DH
        return jnp.where(first, from_first, from_second)

    def masked_heads(x):
        first = lax.broadcasted_iota(jnp.int32, x.shape, 1) < SB_DH
        zero = jnp.zeros_like(x)
        return jnp.where(first, x, zero), jnp.where(first, zero, x)

    def write_out():
        for b, p in units:
            gate = _silu(g_ref[b, :, pair_lanes[p]].astype(F32))
            o_ref[b, :, pair_lanes[p]] = (acc_ref[b, p] * gate).astype(o_ref.dtype)

    def finish(accs, totals):
        gates = {(b, p): g_ref[b, :, pair_lanes[p]] for b, p in units}
        for b, p in units:
            acc_ref[b, p] = accs[b, p]
            total_ref[b, p] = totals[b, p]
            o_ref[b, :, pair_lanes[p]] = (
                accs[b, p] * _silu(gates[b, p].astype(F32))).astype(o_ref.dtype)
        return jnp.max(functools.reduce(jnp.maximum, totals.values()))


    def near_first_blocks():
        row = lax.broadcasted_iota(jnp.int32, (2 * TOK, win), 0) & (TOK - 1)
        col = lax.broadcasted_iota(jnp.int32, (2 * TOK, win), 1)
        strict = (col - row) < i * TOK
        qs = {(b, p): q_ref[b, :, pair_lanes[p]] for b, p in units}
        ks = {(b, p): k_ref[b, 0:win, pair_lanes[p]] for b, p in units}
        vs = {(b, p): v_ref[b, 0:win, pair_lanes[p]] for b, p in units}
        wtri = wtri_ref[0:win, 0:win]
        ys = {u: _dot_nt(jnp.concatenate(masked_heads(qs[u]), axis=0), ks[u]) for u in units}
        log_betas, suffixes, totals, accs = {}, {}, {}, {}
        for u in units:
            log_betas[u], log_rest = _log2_sigmoid_pair(ys[u])
            log_rest = jnp.where(strict, log_rest, 0.0).astype(BF16)
            suffixes[u] = _dot(log_rest, wtri)
            totals[u] = suffixes[u][:, 0:1] + log_rest[:, 0:1].astype(F32)
        for u in units:
            w = jnp.where(strict, jnp.exp2(log_betas[u] + suffixes[u]), 0.0).astype(BF16)
            accs[u] = pick_head(_dot(w[:TOK], vs[u]), _dot(w[TOK:], vs[u]))
        return finish(accs, totals)

    def near():
        subs = TOK // SB_SUB
        rho = lax.broadcasted_iota(jnp.int32, (2 * SB_SUB, LANES), 0) & (SB_SUB - 1)
        col = lax.broadcasted_iota(jnp.int32, (2 * SB_SUB, LANES), 1)
        strict = col < rho + (LANES - SB_SUB)

        def mask(x):
            body = SB_KEYS - LANES
            return jnp.concatenate([x[:, :body], jnp.where(strict, x[:, body:], 0.0)], axis=1)

        tiles = [(b, p, s) for b, p in units for s in range(subs)]
        tri = wtri_ref[0:SB_KEYS, 0:SB_KEYS]
        lhs, ks, vs = {}, {}, {}
        for b, p in units:
            q0, q1 = masked_heads(q_ref[b, :, pair_lanes[p]])
            for s in range(subs):
                sub = slice(s * SB_SUB, (s + 1) * SB_SUB)
                start = pl.multiple_of((i * TOK + (s + 1) * SB_SUB) - SB_KEYS, SB_SUB)
                lhs[b, p, s] = jnp.concatenate([q0[sub], q1[sub]], axis=0)
                ks[b, p, s] = k_ref[b, pl.ds(start, SB_KEYS), pair_lanes[p]]
                vs[b, p, s] = v_ref[b, pl.ds(start, SB_KEYS), pair_lanes[p]]
        ys = {t: _dot_nt(lhs[t], ks[t]) for t in tiles}
        log_betas, suffixes, sub_totals, sub_accs = {}, {}, {}, {}
        for t in tiles:
            log_betas[t], log_rest = _log2_sigmoid_pair(ys[t])
            log_rest = mask(log_rest).astype(BF16)
            suffixes[t] = _dot(log_rest, tri)
            sub_totals[t] = suffixes[t][:, 0:1] + log_rest[:, 0:1].astype(F32)
        for t in tiles:
            w = mask(jnp.exp2(log_betas[t] + suffixes[t])).astype(BF16)
            sub_accs[t] = pick_head(_dot(w[:SB_SUB], vs[t]), _dot(w[SB_SUB:], vs[t]))
        accs = {(b, p): jnp.concatenate([sub_accs[b, p, s] for s in range(subs)], axis=0)
                for b, p in units}
        totals = {(b, p): jnp.concatenate(
            [sub_totals[b, p, s][hh * SB_SUB:(hh + 1) * SB_SUB]
             for hh in range(2) for s in range(subs)], axis=0) for b, p in units}
        return finish(accs, totals)

    @pl.when(i < SB_NEAR - 1)
    def _():
        near_first_blocks()

    @pl.when(i >= SB_NEAR - 1)
    def _():
        top = near()

        @pl.when(top >= F32_EXP2_UNDERFLOW)
        def _():
            tri = tri_ref[...]
            for b, p in units:
                for hh in range(2):
                    carry_ref[b, 2 * p + hh] = jnp.broadcast_to(
                        total_ref[b, p, hh * TOK:(hh + 1) * TOK, :], (TOK, TOK))
            row = lax.broadcasted_iota(jnp.int32, (TOK, TOK), 0)
            col = lax.broadcasted_iota(jnp.int32, (TOK, TOK), 1)
            uncovered = col < (row // SB_SUB + 1) * SB_SUB + (SB_NEAR - 1) * TOK - SB_KEYS

            def visit(j, partly_covered):
                blk = pl.ds(pl.multiple_of(j * TOK, TOK), TOK)
                for b, p in units:
                    qs = masked_heads(q_ref[b, :, pair_lanes[p]])
                    vs = masked_heads(v_ref[b, blk, pair_lanes[p]])
                    kp = k_ref[b, blk, pair_lanes[p]]
                    for hh in range(2):
                        h = 2 * p + hh
                        log_beta, log_rest = _log2_sigmoid_pair(_dot_nt(qs[hh], kp))
                        if partly_covered:
                            log_rest = jnp.where(uncovered, log_rest, 0.0)
                        sums = _dot(log_rest.astype(BF16), tri)
                        carry = carry_ref[b, h]
                        w = jnp.exp2(log_beta + sums[:, :TOK] + carry)
                        if partly_covered:
                            w = jnp.where(uncovered, w, 0.0)
                        acc_ref[b, p] += _dot(w.astype(BF16), vs[hh])
                        carry_ref[b, h] = carry + sums[:, TOK:]

            visit(i - (SB_NEAR - 1), True)

            def cond(state):
                d, far_top = state
                return jnp.logical_and(d <= i, far_top >= F32_EXP2_UNDERFLOW)

            def body(state):
                d, _ = state
                visit(i - d, False)
                return d + 1, jnp.max(carry_ref[...])

            lax.while_loop(cond, body, (jnp.int32(SB_NEAR), jnp.max(carry_ref[...])))
            write_out()


def _stick_breaking(proj, *, bb=4):
    bsz, seq, _ = proj.shape
    w = SB_HEADS * SB_DH
    first = proj.shape[2] // w - 4
    win = max(SB_KEYS, (SB_NEAR - 1) * TOK)
    j = np.arange(win)
    later = (j[:, None] > j[None, :]).astype(np.float32)
    tri = np.concatenate([later[:TOK, :TOK], np.ones((TOK, TOK), np.float32)], axis=1)
    blk = lambda c: pl.BlockSpec((bb, TOK, w), lambda b, i: (b, i, c))
    full = lambda c: pl.BlockSpec((bb, seq, w), lambda b, i: (b, 0, c))
    whole = lambda shape: pl.BlockSpec(shape, lambda b, i: (0,) * len(shape))
    return pl.pallas_call(
        functools.partial(_stick_breaking_kernel, bb=bb),
        grid=(bsz // bb, seq // TOK),
        in_specs=[blk(first), full(first + 1), full(first + 2), blk(first + 3),
                  whole((win, win)), whole((TOK, 2 * TOK))],
        out_specs=pl.BlockSpec((bb, TOK, w), lambda b, i: (b, i, 0)),
        out_shape=jax.ShapeDtypeStruct((bsz, seq, w), BF16),
        scratch_shapes=[pltpu.VMEM((bb, SB_HEADS // 2, TOK, LANES), F32),
                        pltpu.VMEM((bb, SB_HEADS // 2, 2 * TOK, 1), F32),
                        pltpu.VMEM((bb, SB_HEADS, TOK, TOK), F32)],
        compiler_params=pltpu.CompilerParams(dimension_semantics=("arbitrary", "arbitrary")),
        name="stick_breaking",
    )(proj, proj, proj, proj, jnp.asarray(later, dtype=BF16), jnp.asarray(tri, dtype=BF16))


HG_LEVELS = tuple(2 ** e for e in range(int(np.log2(TOK))))
HG_FIRST_VPU_LEVEL = SUBLANES // 2


def _half_boundary(cum3, last_rows, m):
    if 2 * m == SUBLANES:
        return jnp.broadcast_to(cum3[:, m - 1:m, :], cum3.shape)
    per_block = 2 * m // SUBLANES
    picks = [(g // per_block) * per_block + per_block // 2 - 1 for g in range(cum3.shape[0])]
    return jnp.concatenate([last_rows[g:g + 1] for g in picks], axis=0)


def _hgrn_consts():
    t = np.arange(TOK)
    tt, uu = t[:, None], t[None, :]
    mats = [(uu <= tt)]
    pair_masks = [np.eye(TOK, dtype=bool)]
    for m in HG_LEVELS:
        same_block = (tt // (2 * m)) == (uu // (2 * m))
        up_t = (tt % (2 * m)) >= m
        up_u = (uu % (2 * m)) >= m
        if m < HG_FIRST_VPU_LEVEL:
            mats.append(same_block & (up_t == up_u) & np.where(up_t, uu <= tt, uu > tt))
        pair_masks.append(same_block & up_t & ~up_u)
    prefix = np.concatenate(mats, axis=0).astype(np.float32)
    return jnp.asarray(prefix, dtype=BF16), jnp.asarray(np.stack(pair_masks), dtype=F32)


def _hgrn_kernel(q_ref, f_ref, i_ref, g_ref, lbl_ref, prefix_ref, pm_ref, o_ref, state_ref,
                 *, bb, layer):
    @pl.when(pl.program_id(1) == 0)
    def _():
        state_ref[...] = jnp.zeros_like(state_ref)

    logits = lbl_ref[...]
    e = jnp.exp(logits - jnp.max(logits, axis=0, keepdims=True))
    soft = e / jnp.sum(e, axis=0, keepdims=True)
    lb_all = jnp.zeros_like(soft[0:1])
    for r in range(1, layer + 1):
        lb_all = lb_all + soft[r:r + 1]
    x1 = jnp.log(lb_all)
    log_keep = jnp.log(1.0 - lb_all)
    prefix = prefix_ref[...]
    n_lv = len(HG_LEVELS)
    units = [(b, h) for b in range(bb) for h in range(HG_HEADS)]
    hl = lambda h: slice(h * LANES, (h + 1) * LANES)
    qs, kks, vs, states, cums, rests, level_sums = {}, {}, {}, {}, {}, {}, {}
    for b in range(bb):
        for p in range(HG_HEADS // 2):
            lanes = slice(2 * p * LANES, 2 * (p + 1) * LANES)
            fl = f_ref[b, :, lanes].astype(F32)
            ls = _log_sigmoid(fl)
            x2 = log_keep[:, lanes] + ls
            d = x1[:, lanes] - x2
            log_f = jnp.maximum(x1[:, lanes], x2) + jnp.log(1.0 + jnp.exp(jnp.minimum(d, -d)))
            pair_sums = _dot(prefix, (log_f * LOG2E).astype(BF16))
            cum = pair_sums[0:TOK]
            levels = [pair_sums[(1 + lv) * TOK:(2 + lv) * TOK]
                      for lv in range(n_lv) if HG_LEVELS[lv] < HG_FIRST_VPU_LEVEL]
            cum3 = cum.reshape(TOK // SUBLANES, SUBLANES, 2 * LANES)
            last_rows = jnp.broadcast_to(cum3[:, SUBLANES - 1:SUBLANES, :], cum3.shape)
            for m in HG_LEVELS[len(levels):]:
                gap = cum3 - _half_boundary(cum3, last_rows, m)
                levels.append(jnp.minimum(gap, -gap).reshape(TOK, 2 * LANES))
            rest = cum[TOK - 1:TOK, :] - cum
            kk = (1.0 - lb_all[:, lanes]) * jnp.exp(ls - fl)
            for hh in range(2):
                u = (b, 2 * p + hh)
                head = slice(hh * LANES, (hh + 1) * LANES)
                cums[u], rests[u], kks[u] = cum[:, head], rest[:, head], kk[:, head]
                level_sums[u] = [x[:, head] for x in levels]
    for b, h in units:
        qs[b, h] = q_ref[b, :, hl(h)].astype(F32)
        vs[b, h] = i_ref[b, :, hl(h)]
        states[b, h] = state_ref[b, h]
    attns = {}
    for u in units:
        q, kk = qs[u], kks[u]
        attn = _dot_nt(q.astype(BF16), kk.astype(BF16)) * pm_ref[0]
        for lv in range(n_lv):
            dec = jnp.exp2(level_sums[u][lv])
            attn = attn + _dot_nt((q * dec).astype(BF16), (kk * dec).astype(BF16)) * pm_ref[lv + 1]
        attns[u] = attn.astype(BF16)
    outs, kvs = {}, {}
    for u in units:
        ktail = (kks[u] * jnp.exp2(rests[u])).astype(BF16)
        q_in = (qs[u] * jnp.exp2(cums[u])).astype(BF16)
        outs[u] = _dot(attns[u], vs[u]) + _dot_nt(q_in, states[u].astype(BF16))
        kvs[u] = _dot_tn(vs[u], ktail)
    for b, h in units:
        o = outs[b, h]
        state_ref[b, h] = states[b, h] * jnp.exp2(cums[b, h][TOK - 1:TOK, :]) + kvs[b, h]
        ms = jnp.mean(o * o, axis=-1, keepdims=True)
        gate = _silu(g_ref[b, :, hl(h)].astype(F32))
        o_ref[b, :, hl(h)] = (o * lax.rsqrt(ms + EPS) * gate).astype(o_ref.dtype)


def _hgrn(proj, lb_logits, layer, *, bb=8):
    bsz, seq, _ = proj.shape
    w = HG_HEADS * LANES
    prefix, pair_masks = _hgrn_consts()
    blk = lambda c: pl.BlockSpec((bb, TOK, w), lambda b, i: (b, i, c))
    whole = lambda a: pl.BlockSpec(a.shape, lambda b, i: (0,) * a.ndim)
    return pl.pallas_call(
        functools.partial(_hgrn_kernel, bb=bb, layer=layer),
        grid=(bsz // bb, seq // TOK),
        in_specs=[blk(0), blk(1), blk(2), blk(3), whole(lb_logits), whole(prefix),
                  whole(pair_masks)],
        out_specs=pl.BlockSpec((bb, TOK, w), lambda b, i: (b, i, 0)),
        out_shape=jax.ShapeDtypeStruct((bsz, seq, w), BF16),
        scratch_shapes=[pltpu.VMEM((bb, HG_HEADS, LANES, LANES), F32)],
        compiler_params=pltpu.CompilerParams(dimension_semantics=("arbitrary", "arbitrary")),
        name="hgrn2",
    )(proj, proj, proj, proj, lb_logits, prefix, pair_masks)


LRU_TOK = 256
HALO = 8


def _lru_kernel(x_ref, g_ref, cw_ref, cb_ref, wa_ref, ba_ref, wx_ref, bx_ref, lam_ref, o_ref,
                xbuf_ref, h_ref, *, bb):
    c = pl.program_id(1)
    tok = x_ref.shape[1]

    @pl.when(c == 0)
    def _():
        xbuf_ref[...] = jnp.zeros_like(xbuf_ref)
        h_ref[...] = jnp.zeros_like(h_ref)

    lam = lam_ref[...]
    neg_sp = -(jnp.maximum(-lam, 0.0) + jnp.log(1.0 + jnp.exp(-jnp.abs(lam))))
    row = lax.broadcasted_iota(jnp.int32, (tok, LRU_WIDTH), 0)
    first_token = jnp.logical_and(row == 0, c == 0)
    sub_row = lax.broadcasted_iota(jnp.int32, (tok // SUBLANES, SUBLANES, LRU_WIDTH), 1)
    groups = LRU_WIDTH // LANES
    for b in range(bb):
        x = x_ref[b].astype(F32)
        xbuf_ref[b, HALO:HALO + tok, :] = x
        y = cb_ref[...] + x * cw_ref[CONV_W - 1:CONV_W, :]
        for j in range(CONV_W - 1):
            shift = CONV_W - 1 - j
            y = y + xbuf_ref[b, HALO - shift:HALO - shift + tok, :] * cw_ref[j:j + 1, :]
        xbuf_ref[b, 0:HALO, :] = x[tok - HALO:tok, :]
        ra, rx = [], []
        for gidx in range(groups):
            yg = y[:, gidx * LANES:(gidx + 1) * LANES].astype(BF16)
            ra.append(_dot(yg, wa_ref[gidx]))
            rx.append(_dot(yg, wx_ref[gidx]))
        r = jax.nn.sigmoid(jnp.concatenate(ra, axis=1) + ba_ref[...])
        ig = jax.nn.sigmoid(jnp.concatenate(rx, axis=1) + bx_ref[...])
        log_a = LRU_C * r * neg_sp
        a = jnp.exp(log_a)
        sq = -jnp.tanh(log_a) * (1.0 + a * a)
        mult = jnp.where(sq > 0.0, sq * lax.rsqrt(sq), 0.0)
        mult = jnp.where(first_token, 1.0, mult)
        u = mult * ig * y
        a = a.reshape(tok // SUBLANES, SUBLANES, LRU_WIDTH)
        u = u.reshape(tok // SUBLANES, SUBLANES, LRU_WIDTH)
        d = 1
        while d < SUBLANES:
            keep = sub_row >= d
            a_prev = jnp.where(keep, pltpu.roll(a, d, 1), 1.0)
            u_prev = jnp.where(keep, pltpu.roll(u, d, 1), 0.0)
            u = a * u_prev + u
            a = a * a_prev
            d *= 2
        h_in = h_ref[b]
        hs = []
        for j in range(tok // SUBLANES):
            hs.append(u[j] + a[j] * h_in)
            h_in = hs[-1][SUBLANES - 1:SUBLANES, :]
        h_ref[b] = h_in
        hcur = jnp.concatenate(hs, axis=0)
        o_ref[b] = (hcur * _silu(g_ref[b].astype(F32))).astype(o_ref.dtype)


def _block_diag_pairs(w):
    n, bw, _ = w.shape
    z = jnp.zeros((n // 2, bw, bw), w.dtype)
    top = jnp.concatenate([w[0::2], z], axis=2)
    bot = jnp.concatenate([z, w[1::2]], axis=2)
    return jnp.concatenate([top, bot], axis=1).astype(BF16)


def _lru(proj, conv_w, conv_b, w_a, b_a, w_x, b_x, lam, *, bb=4):
    bsz, seq, _ = proj.shape
    w = LRU_WIDTH
    first = proj.shape[2] // w - 2
    row = lambda a: a.reshape(1, w).astype(F32)
    blk = lambda c: pl.BlockSpec((bb, LRU_TOK, w), lambda b, i: (b, i, c))
    whole = lambda shape: pl.BlockSpec(shape, lambda b, i: (0,) * len(shape))
    groups = w // LANES
    return pl.pallas_call(
        functools.partial(_lru_kernel, bb=bb),
        grid=(bsz // bb, seq // LRU_TOK),
        in_specs=[blk(first), blk(first + 1), whole((CONV_W, w)), whole((1, w)),
                  whole((groups, LANES, LANES)), whole((1, w)),
                  whole((groups, LANES, LANES)), whole((1, w)), whole((1, w))],
        out_specs=pl.BlockSpec((bb, LRU_TOK, w), lambda b, i: (b, i, 0)),
        out_shape=jax.ShapeDtypeStruct((bsz, seq, w), BF16),
        scratch_shapes=[pltpu.VMEM((bb, HALO + LRU_TOK, w), F32), pltpu.VMEM((bb, 1, w), F32)],
        compiler_params=pltpu.CompilerParams(dimension_semantics=("arbitrary", "arbitrary")),
        name="rg_lru",
    )(proj, proj, conv_w.astype(F32), row(conv_b), _block_diag_pairs(w_a), row(b_a),
      _block_diag_pairs(w_x), row(b_x), row(lam))


def _rotate_half_columns(w, heads, dk):
    d = w.shape[0]
    wh = w.reshape(d, heads, dk)
    half = dk // 2
    return jnp.concatenate([-wh[..., half:], wh[..., :half]], axis=-1).reshape(d, heads * dk)


def _even_in_weights(w_in):
    qk = RET_HEADS * RET_DK
    wq, wk, rest = w_in[:, :qk], w_in[:, qk:2 * qk], w_in[:, 2 * qk:]
    sq = slice(2 * RET_HEADS * RET_DV, 2 * RET_HEADS * RET_DV + SB_HEADS * SB_DH)
    rest = rest.at[:, sq].multiply(SB_DH ** -0.5 * LOG2E)
    cols = [wq, _rotate_half_columns(wq, RET_HEADS, RET_DK),
            wk, _rotate_half_columns(wk, RET_HEADS, RET_DK), rest]
    return jnp.concatenate(cols, axis=1).astype(BF16)


def kernel(x, pre_norm_w, post_norm_w, even_w_in, even_w_out, odd_w_in, odd_w_out, hgrn_lb_logits,
           conv_w, conv_b, lru_w_a, lru_b_a, lru_w_x, lru_b_x, lru_lambda):
    bsz, seq, d = x.shape
    depth = pre_norm_w.shape[0]
    x2d = x.reshape(bsz * seq, d)

    def in_weights(layer):
        if layer % 2 == 0:
            return _even_in_weights(even_w_in[layer // 2])
        return odd_w_in[layer // 2].astype(BF16)

    proj = _in_proj(x2d, pre_norm_w[0], in_weights(0))
    for layer in range(depth):
        idx = layer // 2
        proj = proj.reshape(bsz, seq, -1)
        if layer % 2 == 0:
            mix_a = _retention(proj)
            mix_b = _stick_breaking(proj)
            w_out = even_w_out[idx]
        else:
            mix_a = _hgrn(proj, hgrn_lb_logits.astype(F32), idx)
            mix_b = _lru(proj, conv_w[idx], conv_b[idx], lru_w_a[idx], lru_b_a[idx],
                         lru_w_x[idx], lru_b_x[idx], lru_lambda[idx])
            w_out = odd_w_out[idx]
        mix_a = mix_a.reshape(bsz * seq, -1)
        mix_b = mix_b.reshape(bsz * seq, -1)
        if layer + 1 < depth:
            x2d, proj = _out_in_proj(mix_a, mix_b, w_out.astype(BF16), post_norm_w[layer], x2d,
                                     pre_norm_w[layer + 1], in_weights(layer + 1))
        else:
            x2d = _out_proj(mix_a, mix_b, w_out.astype(BF16), post_norm_w[layer], x2d)
    return x2d.reshape(bsz, seq, d)
```

```python
import functools

import numpy as np
import jax
import jax.numpy as jnp
from jax import lax
from jax.experimental import pallas as pl
from jax.experimental.pallas import tpu as pltpu

F32 = jnp.float32
BF16 = jnp.bfloat16

EPS = 1e-6
LANES = 128
SUBLANES = 8
TOK = 128
ROPE_BASE = 10000.0
RET_HEADS, RET_DK, RET_DV = 4, 64, 128
SB_HEADS, SB_DH = 8, 64
HG_HEADS = 4
LRU_WIDTH, LRU_BLOCKS, CONV_W, LRU_C = 512, 8, 4, 8.0
LOG2E = float(np.log2(np.e))
F32_EXP2_UNDERFLOW = -150.0


def _dot(a, b):
    return jnp.dot(a, b, preferred_element_type=F32)


def _dot_nt(a, b):
    return lax.dot_general(a, b, (((1,), (1,)), ((), ())), preferred_element_type=F32)


def _dot_tn(a, b):
    return lax.dot_general(a, b, (((0,), (0,)), ((), ())), preferred_element_type=F32)


def _log_sigmoid(z):
    return jnp.minimum(z, 0.0) - jnp.log(1.0 + jnp.exp(-jnp.abs(z)))


def _silu(g):
    return g * jax.nn.sigmoid(g)


def _in_proj_kernel(x_ref, nw_ref, w_ref, o_ref, *, n_chunk):
    x = x_ref[...]
    ms = jnp.mean(x * x, axis=-1, keepdims=True)
    h = (x * lax.rsqrt(ms + EPS) * nw_ref[...]).astype(BF16)
    for c in range(o_ref.shape[1] // n_chunk):
        cols = slice(c * n_chunk, (c + 1) * n_chunk)
        o_ref[:, cols] = _dot(h, w_ref[:, cols]).astype(o_ref.dtype)


def _in_proj(x2d, norm_w, w_bf16, *, tm=512, n_chunk=512):
    m, d = x2d.shape
    n = w_bf16.shape[1]
    return pl.pallas_call(
        functools.partial(_in_proj_kernel, n_chunk=n_chunk),
        grid=(m // tm,),
        in_specs=[
            pl.BlockSpec((tm, d), lambda i: (i, 0)),
            pl.BlockSpec((1, d), lambda i: (0, 0)),
            pl.BlockSpec((d, n), lambda i: (0, 0)),
        ],
        out_specs=pl.BlockSpec((tm, n), lambda i: (i, 0)),
        out_shape=jax.ShapeDtypeStruct((m, n), BF16),
        compiler_params=pltpu.CompilerParams(dimension_semantics=("arbitrary",)),
        name="in_proj",
    )(x2d, norm_w.reshape(1, d), w_bf16)


def _out_proj_kernel(ma_ref, mb_ref, w_ref, nw_ref, x_ref, o_ref):
    half = ma_ref.shape[1]
    y = _dot(ma_ref[...], w_ref[:half, :]) + _dot(mb_ref[...], w_ref[half:, :])
    ms = jnp.mean(y * y, axis=-1, keepdims=True)
    o_ref[...] = x_ref[...] + y * lax.rsqrt(ms + EPS) * nw_ref[...]


def _out_proj(mix_a, mix_b, w_bf16, norm_w, x2d, *, tm=512):
    m, d = x2d.shape
    half = mix_a.shape[1]
    return pl.pallas_call(
        _out_proj_kernel,
        grid=(m // tm,),
        in_specs=[
            pl.BlockSpec((tm, half), lambda i: (i, 0)),
            pl.BlockSpec((tm, half), lambda i: (i, 0)),
            pl.BlockSpec((2 * half, d), lambda i: (0, 0)),
            pl.BlockSpec((1, d), lambda i: (0, 0)),
            pl.BlockSpec((tm, d), lambda i: (i, 0)),
        ],
        out_specs=pl.BlockSpec((tm, d), lambda i: (i, 0)),
        out_shape=jax.ShapeDtypeStruct((m, d), F32),
        compiler_params=pltpu.CompilerParams(dimension_semantics=("arbitrary",)),
        name="out_proj",
    )(mix_a, mix_b, w_bf16, norm_w.reshape(1, d), x2d)


def _out_in_proj_kernel(ma_ref, mb_ref, wo_ref, pw_ref, x_ref, nw_ref, wi_ref, xo_ref, proj_ref,
                        *, n_chunk, splits):
    half = ma_ref.shape[1]
    rows = x_ref.shape[0] // splits
    parts = [slice(s * rows, (s + 1) * rows) for s in range(splits)]
    ys = [_dot(ma_ref[r, :], wo_ref[:half, :]) + _dot(mb_ref[r, :], wo_ref[half:, :])
          for r in parts]
    hs = []
    for r, y in zip(parts, ys):
        ms = jnp.mean(y * y, axis=-1, keepdims=True)
        xn = x_ref[r, :] + y * lax.rsqrt(ms + EPS) * pw_ref[...]
        xo_ref[r, :] = xn
        ms = jnp.mean(xn * xn, axis=-1, keepdims=True)
        hs.append((xn * lax.rsqrt(ms + EPS) * nw_ref[...]).astype(BF16))
    for r, h in zip(parts, hs):
        for c in range(proj_ref.shape[1] // n_chunk):
            cols = slice(c * n_chunk, (c + 1) * n_chunk)
            proj_ref[r, cols] = _dot(h, wi_ref[:, cols]).astype(proj_ref.dtype)


def _out_in_proj(mix_a, mix_b, w_out_bf16, post_w, x2d, pre_w, w_in_bf16, *, tm=512, n_chunk=512,
                 splits=2):
    m, d = x2d.shape
    half = mix_a.shape[1]
    n = w_in_bf16.shape[1]
    tile = lambda width: pl.BlockSpec((tm, width), lambda i: (i, 0))
    whole = lambda shape: pl.BlockSpec(shape, lambda i: (0,) * len(shape))
    return pl.pallas_call(
        functools.partial(_out_in_proj_kernel, n_chunk=n_chunk, splits=splits),
        grid=(m // tm,),
        in_specs=[tile(half), tile(half), whole((2 * half, d)), whole((1, d)), tile(d),
                  whole((1, d)), whole((d, n))],
        out_specs=[tile(d), tile(n)],
        out_shape=[jax.ShapeDtypeStruct((m, d), F32), jax.ShapeDtypeStruct((m, n), BF16)],
        compiler_params=pltpu.CompilerParams(dimension_semantics=("arbitrary",)),
        name="out_in_proj",
    )(mix_a, mix_b, w_out_bf16, post_w.reshape(1, d), x2d, pre_w.reshape(1, d), w_in_bf16)


def _retention_kernel(q_ref, k_ref, v_ref, g_ref, rot_ref, cos_ref, sin_ref,
                      decay_ref, qd_ref, kd_ref, o_ref, state_ref, *, bb, g_chunk):
    @pl.when(pl.program_id(1) == 0)
    def _():
        state_ref[...] = jnp.zeros_like(state_ref)

    cos = cos_ref[...]
    sin = sin_ref[...]
    rot = rot_ref[...]
    q_rot = [_dot(q_ref[b], rot) for b in range(bb)]
    k_rot = [_dot(k_ref[b], rot) for b in range(bb)]
    lane = lax.broadcasted_iota(jnp.int32, (TOK, LANES), 1)
    head_mask = (lane < RET_DK, lane >= RET_DK)
    units = [(b, h) for b in range(bb) for h in range(RET_HEADS)]
    hl = lambda h: slice(h * LANES, (h + 1) * LANES)
    qm, kp, vs, states = {}, {}, {}, {}
    for b in range(bb):
        qr = q_ref[b].astype(F32) * cos + q_rot[b] * sin
        kr = (k_ref[b].astype(F32) * cos + k_rot[b] * sin) * (RET_DK ** -0.5)
        for h in range(RET_HEADS):
            lanes = hl(h // 2)
            qm[b, h] = jnp.where(head_mask[h % 2], qr[:, lanes], 0.0)
            kp[b, h] = kr[:, lanes].astype(BF16)
            vs[b, h] = v_ref[b, :, hl(h)]
            states[b, h] = state_ref[b, h]
    scores = {u: _dot_nt(qm[u].astype(BF16), kp[u]) for u in units}
    inter = {u: _dot((qm[u] * qd_ref[u[1]]).astype(BF16), states[u].astype(BF16)) for u in units}
    kv = {u: _dot_tn(kp[u], (vs[u].astype(F32) * kd_ref[u[1]]).astype(BF16)) for u in units}
    outs = {u: _dot((scores[u] * decay_ref[u[1]]).astype(BF16), vs[u]) + inter[u] for u in units}
    for b, h in units:
        o = outs[b, h]
        state_ref[b, h] = states[b, h] * g_chunk[h] + kv[b, h]
        ms = jnp.mean(o * o, axis=-1, keepdims=True)
        gate = _silu(g_ref[b, :, hl(h)].astype(F32))
        o_ref[b, :, hl(h)] = (o * lax.rsqrt(ms + EPS) * gate).astype(o_ref.dtype)


def _retention_consts(seq):
    h = np.arange(RET_HEADS, dtype=np.float64)
    log_g = np.log(1.0 - 2.0 ** (-5.0 - h))
    pos = np.arange(TOK, dtype=np.float64)
    dist = pos[:, None] - pos[None, :]
    decay = np.where(dist >= 0, np.exp(log_g[:, None, None] * np.maximum(dist, 0.0)), 0.0)
    q_decay = np.exp(log_g[:, None] * (pos + 1.0)[None, :])
    k_decay = np.exp(log_g[:, None] * (TOK - 1.0 - pos)[None, :])
    ones = np.ones((1, 1, LANES))
    g_chunk = tuple(float(v) for v in np.exp(log_g * TOK))
    half = RET_DK // 2
    inv = ROPE_BASE ** (-np.arange(half, dtype=np.float64) / half)
    ang = np.arange(seq, dtype=np.float64)[:, None] * inv[None, :]
    cos = np.tile(np.cos(ang), (1, 2 * RET_HEADS))
    sin = np.tile(np.sin(ang), (1, 2 * RET_HEADS))
    width = RET_HEADS * RET_DK
    col = np.arange(width)
    src = np.where(col % RET_DK < half, col + half, col - half)
    rot = np.zeros((width, width))
    rot[src, col] = np.where(col % RET_DK < half, -1.0, 1.0)
    to = lambda a: jnp.asarray(a, dtype=F32)
    return (jnp.asarray(rot, dtype=BF16), to(cos), to(sin), to(decay),
            to(q_decay[:, :, None] * ones), to(k_decay[:, :, None] * ones), g_chunk)


def _retention(proj, *, bb=8):
    bsz, seq, _ = proj.shape
    rot, cos, sin, decay, qd, kd, g_chunk = _retention_consts(seq)
    qk_w = RET_HEADS * RET_DK
    v_w = RET_HEADS * RET_DV
    qk_spec = lambda j: pl.BlockSpec((bb, TOK, qk_w), lambda b, c: (b, c, j))
    v_spec = lambda j: pl.BlockSpec((bb, TOK, v_w), lambda b, c: (b, c, j))
    tab_spec = pl.BlockSpec((TOK, qk_w), lambda b, c: (c, 0))
    const_spec = pl.BlockSpec((RET_HEADS, TOK, LANES), lambda b, c: (0, 0, 0))
    return pl.pallas_call(
        functools.partial(_retention_kernel, bb=bb, g_chunk=g_chunk),
        grid=(bsz // bb, seq // TOK),
        in_specs=[qk_spec(0), qk_spec(1), v_spec(1), v_spec(2),
                  pl.BlockSpec((qk_w, qk_w), lambda b, c: (0, 0)),
                  tab_spec, tab_spec, const_spec, const_spec, const_spec],
        out_specs=pl.BlockSpec((bb, TOK, v_w), lambda b, c: (b, c, 0)),
        out_shape=jax.ShapeDtypeStruct((bsz, seq, v_w), BF16),
        scratch_shapes=[pltpu.VMEM((bb, RET_HEADS, LANES, RET_DV), F32)],
        compiler_params=pltpu.CompilerParams(dimension_semantics=("arbitrary", "arbitrary")),
        name="retention",
    )(proj, proj, proj, proj, rot, cos, sin, decay, qd, kd)


SB_NEAR = 3
SB_SUB = 64
SB_KEYS = 256


def _log2_sigmoid_pair(y):
    soft = jnp.log(1.0 + jnp.exp2(-jnp.abs(y))) * LOG2E
    log_beta = jnp.minimum(y, 0.0) - soft
    return log_beta, log_beta - y


def _stick_breaking_kernel(q_ref, k_ref, v_ref, g_ref, wtri_ref, tri_ref, o_ref,
                           acc_ref, total_ref, carry_ref, *, bb):
    i = pl.program_id(1)
    n_pairs = SB_HEADS // 2
    win = (SB_NEAR - 1) * TOK
    pair_lanes = [slice(p * LANES, (p + 1) * LANES) for p in range(n_pairs)]
    units = [(b, p) for b in range(bb) for p in range(n_pairs)]

    def pick_head(from_first, from_second):
        first = lax.broadcasted_iota(jnp.int32, from_first.shape, 1) < SB_DH
        return jnp.where(first, from_first, from_second)

    def masked_heads(x):
        first = lax.broadcasted_iota(jnp.int32, x.shape, 1) < SB_DH
        zero = jnp.zeros_like(x)
        return jnp.where(first, x, zero), jnp.where(first, zero, x)

    def write_out():
        for b, p in units:
            gate = _silu(g_ref[b, :, pair_lanes[p]].astype(F32))
            o_ref[b, :, pair_lanes[p]] = (acc_ref[b, p] * gate).astype(o_ref.dtype)

    def finish(accs, totals):
        gates = {(b, p): g_ref[b, :, pair_lanes[p]] for b, p in units}
        for b, p in units:
            acc_ref[b, p] = accs[b, p]
            total_ref[b, p] = totals[b, p]
            o_ref[b, :, pair_lanes[p]] = (
                accs[b, p] * _silu(gates[b, p].astype(F32))).astype(o_ref.dtype)
        return jnp.max(functools.reduce(jnp.maximum, totals.values()))


    def near_first_blocks():
        row = lax.broadcasted_iota(jnp.int32, (2 * TOK, win), 0) & (TOK - 1)
        col = lax.broadcasted_iota(jnp.int32, (2 * TOK, win), 1)
        strict = (col - row) < i * TOK
        qs = {(b, p): q_ref[b, :, pair_lanes[p]] for b, p in units}
        ks = {(b, p): k_ref[b, 0:win, pair_lanes[p]] for b, p in units}
        vs = {(b, p): v_ref[b, 0:win, pair_lanes[p]] for b, p in units}
        wtri = wtri_ref[0:win, 0:win]
        ys = {u: _dot_nt(jnp.concatenate(masked_heads(qs[u]), axis=0), ks[u]) for u in units}
        log_betas, suffixes, totals, accs = {}, {}, {}, {}
        for u in units:
            log_betas[u], log_rest = _log2_sigmoid_pair(ys[u])
            log_rest = jnp.where(strict, log_rest, 0.0).astype(BF16)
            suffixes[u] = _dot(log_rest, wtri)
            totals[u] = suffixes[u][:, 0:1] + log_rest[:, 0:1].astype(F32)
        for u in units:
            w = jnp.where(strict, jnp.exp2(log_betas[u] + suffixes[u]), 0.0).astype(BF16)
            accs[u] = pick_head(_dot(w[:TOK], vs[u]), _dot(w[TOK:], vs[u]))
        return finish(accs, totals)

    def near():
        subs = TOK // SB_SUB
        rho = lax.broadcasted_iota(jnp.int32, (2 * SB_SUB, LANES), 0) & (SB_SUB - 1)
        col = lax.broadcasted_iota(jnp.int32, (2 * SB_SUB, LANES), 1)
        strict = col < rho + (LANES - SB_SUB)

        def mask(x):
            body = SB_KEYS - LANES
            return jnp.concatenate([x[:, :body], jnp.where(strict, x[:, body:], 0.0)], axis=1)

        tiles = [(b, p, s) for b, p in units for s in range(subs)]
        tri = wtri_ref[0:SB_KEYS, 0:SB_KEYS]
        lhs, ks, vs = {}, {}, {}
        for b, p in units:
            q0, q1 = masked_heads(q_ref[b, :, pair_lanes[p]])
            for s in range(subs):
                sub = slice(s * SB_SUB, (s + 1) * SB_SUB)
                start = pl.multiple_of((i * TOK + (s + 1) * SB_SUB) - SB_KEYS, SB_SUB)
                lhs[b, p, s] = jnp.concatenate([q0[sub], q1[sub]], axis=0)
                ks[b, p, s] = k_ref[b, pl.ds(start, SB_KEYS), pair_lanes[p]]
                vs[b, p, s] = v_ref[b, pl.ds(start, SB_KEYS), pair_lanes[p]]
        ys = {t: _dot_nt(lhs[t], ks[t]) for t in tiles}
        log_betas, suffixes, sub_totals, sub_accs = {}, {}, {}, {}
        for t in tiles:
            log_betas[t], log_rest = _log2_sigmoid_pair(ys[t])
            log_rest = mask(log_rest).astype(BF16)
            suffixes[t] = _dot(log_rest, tri)
            sub_totals[t] = suffixes[t][:, 0:1] + log_rest[:, 0:1].astype(F32)
        for t in tiles:
            w = mask(jnp.exp2(log_betas[t] + suffixes[t])).astype(BF16)
            sub_accs[t] = pick_head(_dot(w[:SB_SUB], vs[t]), _dot(w[SB_SUB:], vs[t]))
        accs = {(b, p): jnp.concatenate([sub_accs[b, p, s] for s in range(subs)], axis=0)
                for b, p in units}
        totals = {(b, p): jnp.concatenate(
            [sub_totals[b, p, s][hh * SB_SUB:(hh + 1) * SB_SUB]
             for hh in range(2) for s in range(subs)], axis=0) for b, p in units}
        return finish(accs, totals)

    @pl.when(i < SB_NEAR - 1)
    def _():
        near_first_blocks()

    @pl.when(i >= SB_NEAR - 1)
    def _():
        top = near()

        @pl.when(top >= F32_EXP2_UNDERFLOW)
        def _():
            tri = tri_ref[...]
            for b, p in units:
                for hh in range(2):
                    carry_ref[b, 2 * p + hh] = jnp.broadcast_to(
                        total_ref[b, p, hh * TOK:(hh + 1) * TOK, :], (TOK, TOK))
            row = lax.broadcasted_iota(jnp.int32, (TOK, TOK), 0)
            col = lax.broadcasted_iota(jnp.int32, (TOK, TOK), 1)
            uncovered = col < (row // SB_SUB + 1) * SB_SUB + (SB_NEAR - 1) * TOK - SB_KEYS

            def visit(j, partly_covered):
                blk = pl.ds(pl.multiple_of(j * TOK, TOK), TOK)
                for b, p in units:
                    qs = masked_heads(q_ref[b, :, pair_lanes[p]])
                    vs = masked_heads(v_ref[b, blk, pair_lanes[p]])
                    kp = k_ref[b, blk, pair_lanes[p]]
                    for hh in range(2):
                        h = 2 * p + hh
                        log_beta, log_rest = _log2_sigmoid_pair(_dot_nt(qs[hh], kp))
                        if partly_covered:
                            log_rest = jnp.where(uncovered, log_rest, 0.0)
                        sums = _dot(log_rest.astype(BF16), tri)
                        carry = carry_ref[b, h]
                        w = jnp.exp2(log_beta + sums[:, :TOK] + carry)
                        if partly_covered:
                            w = jnp.where(uncovered, w, 0.0)
                        acc_ref[b, p] += _dot(w.astype(BF16), vs[hh])
                        carry_ref[b, h] = carry + sums[:, TOK:]

            visit(i - (SB_NEAR - 1), True)

            def cond(state):
                d, far_top = state
                return jnp.logical_and(d <= i, far_top >= F32_EXP2_UNDERFLOW)

            def body(state):
                d, _ = state
                visit(i - d, False)
                return d + 1, jnp.max(carry_ref[...])

            lax.while_loop(cond, body, (jnp.int32(SB_NEAR), jnp.max(carry_ref[...])))
            write_out()


def _stick_breaking(proj, *, bb=4):
    bsz, seq, _ = proj.shape
    w = SB_HEADS * SB_DH
    first = proj.shape[2] // w - 4
    win = max(SB_KEYS, (SB_NEAR - 1) * TOK)
    j = np.arange(win)
    later = (j[:, None] > j[None, :]).astype(np.float32)
    tri = np.concatenate([later[:TOK, :TOK], np.ones((TOK, TOK), np.float32)], axis=1)
    blk = lambda c: pl.BlockSpec((bb, TOK, w), lambda b, i: (b, i, c))
    full = lambda c: pl.BlockSpec((bb, seq, w), lambda b, i: (b, 0, c))
    whole = lambda shape: pl.BlockSpec(shape, lambda b, i: (0,) * len(shape))
    return pl.pallas_call(
        functools.partial(_stick_breaking_kernel, bb=bb),
        grid=(bsz // bb, seq // TOK),
        in_specs=[blk(first), full(first + 1), full(first + 2), blk(first + 3),
                  whole((win, win)), whole((TOK, 2 * TOK))],
        out_specs=pl.BlockSpec((bb, TOK, w), lambda b, i: (b, i, 0)),
        out_shape=jax.ShapeDtypeStruct((bsz, seq, w), BF16),
        scratch_shapes=[pltpu.VMEM((bb, SB_HEADS // 2, TOK, LANES), F32),
                        pltpu.VMEM((bb, SB_HEADS // 2, 2 * TOK, 1), F32),
                        pltpu.VMEM((bb, SB_HEADS, TOK, TOK), F32)],
        compiler_params=pltpu.CompilerParams(dimension_semantics=("arbitrary", "arbitrary")),
        name="stick_breaking",
    )(proj, proj, proj, proj, jnp.asarray(later, dtype=BF16), jnp.asarray(tri, dtype=BF16))


HG_LEVELS = tuple(2 ** e for e in range(int(np.log2(TOK))))
HG_FIRST_VPU_LEVEL = SUBLANES // 2


def _half_boundary(cum3, last_rows, m):
    if 2 * m == SUBLANES:
        return jnp.broadcast_to(cum3[:, m - 1:m, :], cum3.shape)
    per_block = 2 * m // SUBLANES
    picks = [(g // per_block) * per_block + per_block // 2 - 1 for g in range(cum3.shape[0])]
    return jnp.concatenate([last_rows[g:g + 1] for g in picks], axis=0)


def _hgrn_consts():
    t = np.arange(TOK)
    tt, uu = t[:, None], t[None, :]
    mats = [(uu <= tt)]
    pair_masks = [np.eye(TOK, dtype=bool)]
    for m in HG_LEVELS:
        same_block = (tt // (2 * m)) == (uu // (2 * m))
        up_t = (tt % (2 * m)) >= m
        up_u = (uu % (2 * m)) >= m
        if m < HG_FIRST_VPU_LEVEL:
            mats.append(same_block & (up_t == up_u) & np.where(up_t, uu <= tt, uu > tt))
        pair_masks.append(same_block & up_t & ~up_u)
    prefix = np.concatenate(mats, axis=0).astype(np.float32)
    return jnp.asarray(prefix, dtype=BF16), jnp.asarray(np.stack(pair_masks), dtype=F32)


def _hgrn_kernel(q_ref, f_ref, i_ref, g_ref, lbl_ref, prefix_ref, pm_ref, o_ref, state_ref,
                 *, bb, layer):
    @pl.when(pl.program_id(1) == 0)
    def _():
        state_ref[...] = jnp.zeros_like(state_ref)

    logits = lbl_ref[...]
    e = jnp.exp(logits - jnp.max(logits, axis=0, keepdims=True))
    soft = e / jnp.sum(e, axis=0, keepdims=True)
    lb_all = jnp.zeros_like(soft[0:1])
    for r in range(1, layer + 1):
        lb_all = lb_all + soft[r:r + 1]
    x1 = jnp.log(lb_all)
    log_keep = jnp.log(1.0 - lb_all)
    prefix = prefix_ref[...]
    n_lv = len(HG_LEVELS)
    units = [(b, h) for b in range(bb) for h in range(HG_HEADS)]
    hl = lambda h: slice(h * LANES, (h + 1) * LANES)
    qs, kks, vs, states, cums, rests, level_sums = {}, {}, {}, {}, {}, {}, {}
    for b in range(bb):
        for p in range(HG_HEADS // 2):
            lanes = slice(2 * p * LANES, 2 * (p + 1) * LANES)
            fl = f_ref[b, :, lanes].astype(F32)
            ls = _log_sigmoid(fl)
            x2 = log_keep[:, lanes] + ls
            d = x1[:, lanes] - x2
            log_f = jnp.maximum(x1[:, lanes], x2) + jnp.log(1.0 + jnp.exp(jnp.minimum(d, -d)))
            pair_sums = _dot(prefix, (log_f * LOG2E).astype(BF16))
            cum = pair_sums[0:TOK]
            levels = [pair_sums[(1 + lv) * TOK:(2 + lv) * TOK]
                      for lv in range(n_lv) if HG_LEVELS[lv] < HG_FIRST_VPU_LEVEL]
            cum3 = cum.reshape(TOK // SUBLANES, SUBLANES, 2 * LANES)
            last_rows = jnp.broadcast_to(cum3[:, SUBLANES - 1:SUBLANES, :], cum3.shape)
            for m in HG_LEVELS[len(levels):]:
                gap = cum3 - _half_boundary(cum3, last_rows, m)
                levels.append(jnp.minimum(gap, -gap).reshape(TOK, 2 * LANES))
            rest = cum[TOK - 1:TOK, :] - cum
            kk = (1.0 - lb_all[:, lanes]) * jnp.exp(ls - fl)
            for hh in range(2):
                u = (b, 2 * p + hh)
                head = slice(hh * LANES, (hh + 1) * LANES)
                cums[u], rests[u], kks[u] = cum[:, head], rest[:, head], kk[:, head]
                level_sums[u] = [x[:, head] for x in levels]
    for b, h in units:
        qs[b, h] = q_ref[b, :, hl(h)].astype(F32)
        vs[b, h] = i_ref[b, :, hl(h)]
        states[b, h] = state_ref[b, h]
    attns = {}
    for u in units:
        q, kk = qs[u], kks[u]
        attn = _dot_nt(q.astype(BF16), kk.astype(BF16)) * pm_ref[0]
        for lv in range(n_lv):
            dec = jnp.exp2(level_sums[u][lv])
            attn = attn + _dot_nt((q * dec).astype(BF16), (kk * dec).astype(BF16)) * pm_ref[lv + 1]
        attns[u] = attn.astype(BF16)
    outs, kvs = {}, {}
    for u in units:
        ktail = (kks[u] * jnp.exp2(rests[u])).astype(BF16)
        q_in = (qs[u] * jnp.exp2(cums[u])).astype(BF16)
        outs[u] = _dot(attns[u], vs[u]) + _dot_nt(q_in, states[u].astype(BF16))
        kvs[u] = _dot_tn(vs[u], ktail)
    for b, h in units:
        o = outs[b, h]
        state_ref[b, h] = states[b, h] * jnp.exp2(cums[b, h][TOK - 1:TOK, :]) + kvs[b, h]
        ms = jnp.mean(o * o, axis=-1, keepdims=True)
        gate = _silu(g_ref[b, :, hl(h)].astype(F32))
        o_ref[b, :, hl(h)] = (o * lax.rsqrt(ms + EPS) * gate).astype(o_ref.dtype)


def _hgrn(proj, lb_logits, layer, *, bb=8):
    bsz, seq, _ = proj.shape
    w = HG_HEADS * LANES
    prefix, pair_masks = _hgrn_consts()
    blk = lambda c: pl.BlockSpec((bb, TOK, w), lambda b, i: (b, i, c))
    whole = lambda a: pl.BlockSpec(a.shape, lambda b, i: (0,) * a.ndim)
    return pl.pallas_call(
        functools.partial(_hgrn_kernel, bb=bb, layer=layer),
        grid=(bsz // bb, seq // TOK),
        in_specs=[blk(0), blk(1), blk(2), blk(3), whole(lb_logits), whole(prefix),
                  whole(pair_masks)],
        out_specs=pl.BlockSpec((bb, TOK, w), lambda b, i: (b, i, 0)),
        out_shape=jax.ShapeDtypeStruct((bsz, seq, w), BF16),
        scratch_shapes=[pltpu.VMEM((bb, HG_HEADS, LANES, LANES), F32)],
        compiler_params=pltpu.CompilerParams(dimension_semantics=("arbitrary", "arbitrary")),
        name="hgrn2",
    )(proj, proj, proj, proj, lb_logits, prefix, pair_masks)


LRU_TOK = 256
HALO = 8


def _lru_kernel(x_ref, g_ref, cw_ref, cb_ref, wa_ref, ba_ref, wx_ref, bx_ref, lam_ref, o_ref,
                xbuf_ref, h_ref, *, bb):
    c = pl.program_id(1)
    tok = x_ref.shape[1]

    @pl.when(c == 0)
    def _():
        xbuf_ref[...] = jnp.zeros_like(xbuf_ref)
        h_ref[...] = jnp.zeros_like(h_ref)

    lam = lam_ref[...]
    neg_sp = -(jnp.maximum(-lam, 0.0) + jnp.log(1.0 + jnp.exp(-jnp.abs(lam))))
    row = lax.broadcasted_iota(jnp.int32, (tok, LRU_WIDTH), 0)
    first_token = jnp.logical_and(row == 0, c == 0)
    sub_row = lax.broadcasted_iota(jnp.int32, (tok // SUBLANES, SUBLANES, LRU_WIDTH), 1)
    groups = LRU_WIDTH // LANES
    for b in range(bb):
        x = x_ref[b].astype(F32)
        xbuf_ref[b, HALO:HALO + tok, :] = x
        y = cb_ref[...] + x * cw_ref[CONV_W - 1:CONV_W, :]
        for j in range(CONV_W - 1):
            shift = CONV_W - 1 - j
            y = y + xbuf_ref[b, HALO - shift:HALO - shift + tok, :] * cw_ref[j:j + 1, :]
        xbuf_ref[b, 0:HALO, :] = x[tok - HALO:tok, :]
        ra, rx = [], []
        for gidx in range(groups):
            yg = y[:, gidx * LANES:(gidx + 1) * LANES].astype(BF16)
            ra.append(_dot(yg, wa_ref[gidx]))
            rx.append(_dot(yg, wx_ref[gidx]))
        r = jax.nn.sigmoid(jnp.concatenate(ra, axis=1) + ba_ref[...])
        ig = jax.nn.sigmoid(jnp.concatenate(rx, axis=1) + bx_ref[...])
        log_a = LRU_C * r * neg_sp
        a = jnp.exp(log_a)
        sq = -jnp.tanh(log_a) * (1.0 + a * a)
        mult = jnp.where(sq > 0.0, sq * lax.rsqrt(sq), 0.0)
        mult = jnp.where(first_token, 1.0, mult)
        u = mult * ig * y
        a = a.reshape(tok // SUBLANES, SUBLANES, LRU_WIDTH)
        u = u.reshape(tok // SUBLANES, SUBLANES, LRU_WIDTH)
        d = 1
        while d < SUBLANES:
            keep = sub_row >= d
            a_prev = jnp.where(keep, pltpu.roll(a, d, 1), 1.0)
            u_prev = jnp.where(keep, pltpu.roll(u, d, 1), 0.0)
            u = a * u_prev + u
            a = a * a_prev
            d *= 2
        h_in = h_ref[b]
        hs = []
        for j in range(tok // SUBLANES):
            hs.append(u[j] + a[j] * h_in)
            h_in = hs[-1][SUBLANES - 1:SUBLANES, :]
        h_ref[b] = h_in
        hcur = jnp.concatenate(hs, axis=0)
        o_ref[b] = (hcur * _silu(g_ref[b].astype(F32))).astype(o_ref.dtype)


def _block_diag_pairs(w):
    n, bw, _ = w.shape
    z = jnp.zeros((n // 2, bw, bw), w.dtype)
    top = jnp.concatenate([w[0::2], z], axis=2)
    bot = jnp.concatenate([z, w[1::2]], axis=2)
    return jnp.concatenate([top, bot], axis=1).astype(BF16)


def _lru(proj, conv_w, conv_b, w_a, b_a, w_x, b_x, lam, *, bb=4):
    bsz, seq, _ = proj.shape
    w = LRU_WIDTH
    first = proj.shape[2] // w - 2
    row = lambda a: a.reshape(1, w).astype(F32)
    blk = lambda c: pl.BlockSpec((bb, LRU_TOK, w), lambda b, i: (b, i, c))
    whole = lambda shape: pl.BlockSpec(shape, lambda b, i: (0,) * len(shape))
    groups = w // LANES
    return pl.pallas_call(
        functools.partial(_lru_kernel, bb=bb),
        grid=(bsz // bb, seq // LRU_TOK),
        in_specs=[blk(first), blk(first + 1), whole((CONV_W, w)), whole((1, w)),
                  whole((groups, LANES, LANES)), whole((1, w)),
                  whole((groups, LANES, LANES)), whole((1, w)), whole((1, w))],
        out_specs=pl.BlockSpec((bb, LRU_TOK, w), lambda b, i: (b, i, 0)),
        out_shape=jax.ShapeDtypeStruct((bsz, seq, w), BF16),
        scratch_shapes=[pltpu.VMEM((bb, HALO + LRU_TOK, w), F32), pltpu.VMEM((bb, 1, w), F32)],
        compiler_params=pltpu.CompilerParams(dimension_semantics=("arbitrary", "arbitrary")),
        name="rg_lru",
    )(proj, proj, conv_w.astype(F32), row(conv_b), _block_diag_pairs(w_a), row(b_a),
      _block_diag_pairs(w_x), row(b_x), row(lam))


def _even_in_weights(w_in):
    sq_first = 2 * RET_HEADS * RET_DK + 2 * RET_HEADS * RET_DV
    scale = np.ones((1, w_in.shape[1]), np.float32)
    scale[:, sq_first:sq_first + SB_HEADS * SB_DH] = SB_DH ** -0.5 * LOG2E
    return (w_in * scale).astype(BF16)


def kernel(x, pre_norm_w, post_norm_w, even_w_in, even_w_out, odd_w_in, odd_w_out, hgrn_lb_logits,
           conv_w, conv_b, lru_w_a, lru_b_a, lru_w_x, lru_b_x, lru_lambda):
    bsz, seq, d = x.shape
    depth = pre_norm_w.shape[0]
    x2d = x.reshape(bsz * seq, d)

    def in_weights(layer):
        if layer % 2 == 0:
            return _even_in_weights(even_w_in[layer // 2])
        return odd_w_in[layer // 2].astype(BF16)

    proj = _in_proj(x2d, pre_norm_w[0], in_weights(0))
    for layer in range(depth):
        idx = layer // 2
        proj = proj.reshape(bsz, seq, -1)
        if layer % 2 == 0:
            mix_a = _retention(proj)
            mix_b = _stick_breaking(proj)
            w_out = even_w_out[idx]
        else:
            mix_a = _hgrn(proj, hgrn_lb_logits.astype(F32), idx)
            mix_b = _lru(proj, conv_w[idx], conv_b[idx], lru_w_a[idx], lru_b_a[idx],
                         lru_w_x[idx], lru_b_x[idx], lru_lambda[idx])
            w_out = odd_w_out[idx]
        mix_a = mix_a.reshape(bsz * seq, -1)
        mix_b = mix_b.reshape(bsz * seq, -1)
        if layer + 1 < depth:
            x2d, proj = _out_in_proj(mix_a, mix_b, w_out.astype(BF16), post_norm_w[layer], x2d,
                                     pre_norm_w[layer + 1], in_weights(layer + 1))
        else:
            x2d = _out_proj(mix_a, mix_b, w_out.astype(BF16), post_norm_w[layer], x2d)
    return x2d.reshape(bsz, seq, d)
```

```python
import functools

import numpy as np
import jax
import jax.numpy as jnp
from jax import lax
from jax.experimental import pallas as pl
from jax.experimental.pallas import tpu as pltpu

F32 = jnp.float32
BF16 = jnp.bfloat16

EPS = 1e-6
LANES = 128
SUBLANES = 8
TOK = 128
ROPE_BASE = 10000.0
RET_HEADS, RET_DK, RET_DV = 4, 64, 128
SB_HEADS, SB_DH = 8, 64
HG_HEADS = 4
LRU_WIDTH, LRU_BLOCKS, CONV_W, LRU_C = 512, 8, 4, 8.0
LOG2E = float(np.log2(np.e))
F32_EXP2_UNDERFLOW = -150.0


def _dot(a, b):
    return jnp.dot(a, b, preferred_element_type=F32)


def _dot_nt(a, b):
    return lax.dot_general(a, b, (((1,), (1,)), ((), ())), preferred_element_type=F32)


def _dot_tn(a, b):
    return lax.dot_general(a, b, (((0,), (0,)), ((), ())), preferred_element_type=F32)


def _log_sigmoid(z):
    return jnp.minimum(z, 0.0) - jnp.log(1.0 + jnp.exp(-jnp.abs(z)))


def _silu(g):
    return g * jax.nn.sigmoid(g)


def _in_proj_kernel(x_ref, nw_ref, w_ref, o_ref, *, n_chunk):
    x = x_ref[...]
    ms = jnp.mean(x * x, axis=-1, keepdims=True)
    h = (x * lax.rsqrt(ms + EPS) * nw_ref[...]).astype(BF16)
    for c in range(o_ref.shape[1] // n_chunk):
        cols = slice(c * n_chunk, (c + 1) * n_chunk)
        o_ref[:, cols] = _dot(h, w_ref[:, cols]).astype(o_ref.dtype)


def _in_proj(x2d, norm_w, w_bf16, *, tm=512, n_chunk=512):
    m, d = x2d.shape
    n = w_bf16.shape[1]
    return pl.pallas_call(
        functools.partial(_in_proj_kernel, n_chunk=n_chunk),
        grid=(m // tm,),
        in_specs=[
            pl.BlockSpec((tm, d), lambda i: (i, 0)),
            pl.BlockSpec((1, d), lambda i: (0, 0)),
            pl.BlockSpec((d, n), lambda i: (0, 0)),
        ],
        out_specs=pl.BlockSpec((tm, n), lambda i: (i, 0)),
        out_shape=jax.ShapeDtypeStruct((m, n), BF16),
        compiler_params=pltpu.CompilerParams(dimension_semantics=("arbitrary",)),
        name="in_proj",
    )(x2d, norm_w.reshape(1, d), w_bf16)


def _out_proj_kernel(ma_ref, mb_ref, w_ref, nw_ref, x_ref, o_ref):
    half = ma_ref.shape[1]
    y = _dot(ma_ref[...], w_ref[:half, :]) + _dot(mb_ref[...], w_ref[half:, :])
    ms = jnp.mean(y * y, axis=-1, keepdims=True)
    o_ref[...] = x_ref[...] + y * lax.rsqrt(ms + EPS) * nw_ref[...]


def _out_proj(mix_a, mix_b, w_bf16, norm_w, x2d, *, tm=1024):
    m, d = x2d.shape
    half = mix_a.shape[1]
    return pl.pallas_call(
        _out_proj_kernel,
        grid=(m // tm,),
        in_specs=[
            pl.BlockSpec((tm, half), lambda i: (i, 0)),
            pl.BlockSpec((tm, half), lambda i: (i, 0)),
            pl.BlockSpec((2 * half, d), lambda i: (0, 0)),
            pl.BlockSpec((1, d), lambda i: (0, 0)),
            pl.BlockSpec((tm, d), lambda i: (i, 0)),
        ],
        out_specs=pl.BlockSpec((tm, d), lambda i: (i, 0)),
        out_shape=jax.ShapeDtypeStruct((m, d), F32),
        compiler_params=pltpu.CompilerParams(dimension_semantics=("arbitrary",)),
        name="out_proj",
    )(mix_a, mix_b, w_bf16, norm_w.reshape(1, d), x2d)


def _out_in_proj_kernel(ma_ref, mb_ref, wo_ref, pw_ref, x_ref, nw_ref, wi_ref, xo_ref, proj_ref,
                        *, n_chunk, splits):
    half = ma_ref.shape[1]
    rows = x_ref.shape[0] // splits
    parts = [slice(s * rows, (s + 1) * rows) for s in range(splits)]
    ys = [_dot(ma_ref[r, :], wo_ref[:half, :]) + _dot(mb_ref[r, :], wo_ref[half:, :])
          for r in parts]
    hs = []
    for r, y in zip(parts, ys):
        ms = jnp.mean(y * y, axis=-1, keepdims=True)
        xn = x_ref[r, :] + y * lax.rsqrt(ms + EPS) * pw_ref[...]
        xo_ref[r, :] = xn
        ms = jnp.mean(xn * xn, axis=-1, keepdims=True)
        hs.append((xn * lax.rsqrt(ms + EPS) * nw_ref[...]).astype(BF16))
    for r, h in zip(parts, hs):
        for c in range(proj_ref.shape[1] // n_chunk):
            cols = slice(c * n_chunk, (c + 1) * n_chunk)
            proj_ref[r, cols] = _dot(h, wi_ref[:, cols]).astype(proj_ref.dtype)


def _out_in_proj(mix_a, mix_b, w_out_bf16, post_w, x2d, pre_w, w_in_bf16, *, tm=512, n_chunk=512,
                 splits=2):
    m, d = x2d.shape
    half = mix_a.shape[1]
    n = w_in_bf16.shape[1]
    tile = lambda width: pl.BlockSpec((tm, width), lambda i: (i, 0))
    whole = lambda shape: pl.BlockSpec(shape, lambda i: (0,) * len(shape))
    return pl.pallas_call(
        functools.partial(_out_in_proj_kernel, n_chunk=n_chunk, splits=splits),
        grid=(m // tm,),
        in_specs=[tile(half), tile(half), whole((2 * half, d)), whole((1, d)), tile(d),
                  whole((1, d)), whole((d, n))],
        out_specs=[tile(d), tile(n)],
        out_shape=[jax.ShapeDtypeStruct((m, d), F32), jax.ShapeDtypeStruct((m, n), BF16)],
        compiler_params=pltpu.CompilerParams(dimension_semantics=("arbitrary",)),
        name="out_in_proj",
    )(mix_a, mix_b, w_out_bf16, post_w.reshape(1, d), x2d, pre_w.reshape(1, d), w_in_bf16)


def _retention_kernel(q_ref, k_ref, v_ref, g_ref, rot_ref, cos_ref, sin_ref,
                      decay_ref, qd_ref, kd_ref, o_ref, state_ref, *, bb, g_chunk):
    @pl.when(pl.program_id(1) == 0)
    def _():
        state_ref[...] = jnp.zeros_like(state_ref)

    cos = cos_ref[...]
    sin = sin_ref[...]
    rot = rot_ref[...]
    q_rot = [_dot(q_ref[b], rot) for b in range(bb)]
    k_rot = [_dot(k_ref[b], rot) for b in range(bb)]
    lane = lax.broadcasted_iota(jnp.int32, (TOK, LANES), 1)
    head_mask = (lane < RET_DK, lane >= RET_DK)
    units = [(b, h) for b in range(bb) for h in range(RET_HEADS)]
    hl = lambda h: slice(h * LANES, (h + 1) * LANES)
    qm, kp, vs, states = {}, {}, {}, {}
    for b in range(bb):
        qr = q_ref[b].astype(F32) * cos + q_rot[b] * sin
        kr = (k_ref[b].astype(F32) * cos + k_rot[b] * sin) * (RET_DK ** -0.5)
        for h in range(RET_HEADS):
            lanes = hl(h // 2)
            qm[b, h] = jnp.where(head_mask[h % 2], qr[:, lanes], 0.0)
            kp[b, h] = kr[:, lanes].astype(BF16)
            vs[b, h] = v_ref[b, :, hl(h)]
            states[b, h] = state_ref[b, h]
    scores = {u: _dot_nt(qm[u].astype(BF16), kp[u]) for u in units}
    inter = {u: _dot((qm[u] * qd_ref[u[1]]).astype(BF16), states[u].astype(BF16)) for u in units}
    kv = {u: _dot_tn(kp[u], (vs[u].astype(F32) * kd_ref[u[1]]).astype(BF16)) for u in units}
    outs = {u: _dot((scores[u] * decay_ref[u[1]]).astype(BF16), vs[u]) + inter[u] for u in units}
    for b, h in units:
        o = outs[b, h]
        state_ref[b, h] = states[b, h] * g_chunk[h] + kv[b, h]
        ms = jnp.mean(o * o, axis=-1, keepdims=True)
        gate = _silu(g_ref[b, :, hl(h)].astype(F32))
        o_ref[b, :, hl(h)] = (o * lax.rsqrt(ms + EPS) * gate).astype(o_ref.dtype)


def _retention_consts(seq):
    h = np.arange(RET_HEADS, dtype=np.float64)
    log_g = np.log(1.0 - 2.0 ** (-5.0 - h))
    pos = np.arange(TOK, dtype=np.float64)
    dist = pos[:, None] - pos[None, :]
    decay = np.where(dist >= 0, np.exp(log_g[:, None, None] * np.maximum(dist, 0.0)), 0.0)
    q_decay = np.exp(log_g[:, None] * (pos + 1.0)[None, :])
    k_decay = np.exp(log_g[:, None] * (TOK - 1.0 - pos)[None, :])
    ones = np.ones((1, 1, LANES))
    g_chunk = tuple(float(v) for v in np.exp(log_g * TOK))
    half = RET_DK // 2
    inv = ROPE_BASE ** (-np.arange(half, dtype=np.float64) / half)
    ang = np.arange(seq, dtype=np.float64)[:, None] * inv[None, :]
    cos = np.tile(np.cos(ang), (1, 2 * RET_HEADS))
    sin = np.tile(np.sin(ang), (1, 2 * RET_HEADS))
    width = RET_HEADS * RET_DK
    col = np.arange(width)
    src = np.where(col % RET_DK < half, col + half, col - half)
    rot = np.zeros((width, width))
    rot[src, col] = np.where(col % RET_DK < half, -1.0, 1.0)
    to = lambda a: jnp.asarray(a, dtype=F32)
    return (jnp.asarray(rot, dtype=BF16), to(cos), to(sin), to(decay),
            to(q_decay[:, :, None] * ones), to(k_decay[:, :, None] * ones), g_chunk)


def _retention(proj, *, bb=8):
    bsz, seq, _ = proj.shape
    rot, cos, sin, decay, qd, kd, g_chunk = _retention_consts(seq)
    qk_w = RET_HEADS * RET_DK
    v_w = RET_HEADS * RET_DV
    qk_spec = lambda j: pl.BlockSpec((bb, TOK, qk_w), lambda b, c: (b, c, j))
    v_spec = lambda j: pl.BlockSpec((bb, TOK, v_w), lambda b, c: (b, c, j))
    tab_spec = pl.BlockSpec((TOK, qk_w), lambda b, c: (c, 0))
    const_spec = pl.BlockSpec((RET_HEADS, TOK, LANES), lambda b, c: (0, 0, 0))
    return pl.pallas_call(
        functools.partial(_retention_kernel, bb=bb, g_chunk=g_chunk),
        grid=(bsz // bb, seq // TOK),
        in_specs=[qk_spec(0), qk_spec(1), v_spec(1), v_spec(2),
                  pl.BlockSpec((qk_w, qk_w), lambda b, c: (0, 0)),
                  tab_spec, tab_spec, const_spec, const_spec, const_spec],
        out_specs=pl.BlockSpec((bb, TOK, v_w), lambda b, c: (b, c, 0)),
        out_shape=jax.ShapeDtypeStruct((bsz, seq, v_w), BF16),
        scratch_shapes=[pltpu.VMEM((bb, RET_HEADS, LANES, RET_DV), F32)],
        compiler_params=pltpu.CompilerParams(dimension_semantics=("arbitrary", "arbitrary")),
        name="retention",
    )(proj, proj, proj, proj, rot, cos, sin, decay, qd, kd)


SB_NEAR = 3
SB_SUB = 64
SB_KEYS = 256


def _log2_sigmoid_pair(y, dtype=F32):
    y = y.astype(dtype)
    soft = jnp.log(1.0 + jnp.exp2(-jnp.abs(y))) * LOG2E
    log_beta = jnp.minimum(y, 0.0) - soft
    return log_beta, log_beta - y


def _stick_breaking_kernel(q_ref, k_ref, v_ref, g_ref, wtri_ref, tri_ref, o_ref,
                           acc_ref, total_ref, carry_ref, *, bb):
    i = pl.program_id(1)
    n_pairs = SB_HEADS // 2
    win = (SB_NEAR - 1) * TOK
    pair_lanes = [slice(p * LANES, (p + 1) * LANES) for p in range(n_pairs)]
    units = [(b, p) for b in range(bb) for p in range(n_pairs)]

    def pick_head(from_first, from_second):
        first = lax.broadcasted_iota(jnp.int32, from_first.shape, 1) < SB_DH
        return jnp.where(first, from_first, from_second)

    def masked_heads(x):
        first = lax.broadcasted_iota(jnp.int32, x.shape, 1) < SB_DH
        zero = jnp.zeros_like(x)
        return jnp.where(first, x, zero), jnp.where(first, zero, x)

    def write_out():
        for b, p in units:
            gate = _silu(g_ref[b, :, pair_lanes[p]].astype(F32))
            o_ref[b, :, pair_lanes[p]] = (acc_ref[b, p] * gate).astype(o_ref.dtype)

    def finish(accs, totals):
        gates = {(b, p): g_ref[b, :, pair_lanes[p]] for b, p in units}
        for b, p in units:
            acc_ref[b, p] = accs[b, p]
            total_ref[b, p] = totals[b, p]
            o_ref[b, :, pair_lanes[p]] = (
                accs[b, p] * _silu(gates[b, p].astype(F32))).astype(o_ref.dtype)
        return jnp.max(functools.reduce(jnp.maximum, totals.values()))


    def near_first_blocks():
        row = lax.broadcasted_iota(jnp.int32, (2 * TOK, win), 0) & (TOK - 1)
        col = lax.broadcasted_iota(jnp.int32, (2 * TOK, win), 1)
        strict = (col - row) < i * TOK
        qs = {(b, p): q_ref[b, :, pair_lanes[p]] for b, p in units}
        ks = {(b, p): k_ref[b, 0:win, pair_lanes[p]] for b, p in units}
        vs = {(b, p): v_ref[b, 0:win, pair_lanes[p]] for b, p in units}
        wtri = wtri_ref[0:win, 0:win]
        ys = {u: _dot_nt(jnp.concatenate(masked_heads(qs[u]), axis=0), ks[u]) for u in units}
        log_betas, suffixes, totals, accs = {}, {}, {}, {}
        for u in units:
            log_betas[u], log_rest = _log2_sigmoid_pair(ys[u], BF16)
            log_rest = jnp.where(strict, log_rest, 0.0)
            suffixes[u] = _dot(log_rest, wtri)
            totals[u] = suffixes[u][:, 0:1] + log_rest[:, 0:1].astype(F32)
        for u in units:
            w = jnp.where(strict, jnp.exp2(log_betas[u] + suffixes[u].astype(BF16)), 0.0)
            accs[u] = pick_head(_dot(w[:TOK], vs[u]), _dot(w[TOK:], vs[u]))
        return finish(accs, totals)

    def near():
        subs = TOK // SB_SUB
        rho = lax.broadcasted_iota(jnp.int32, (2 * SB_SUB, LANES), 0) & (SB_SUB - 1)
        col = lax.broadcasted_iota(jnp.int32, (2 * SB_SUB, LANES), 1)
        strict = col < rho + (LANES - SB_SUB)

        def mask(x):
            body = SB_KEYS - LANES
            return jnp.concatenate([x[:, :body], jnp.where(strict, x[:, body:], 0.0)], axis=1)

        tiles = [(b, p, s) for b, p in units for s in range(subs)]
        tri = wtri_ref[0:SB_KEYS, 0:SB_KEYS]
        lhs, ks, vs = {}, {}, {}
        for b, p in units:
            q0, q1 = masked_heads(q_ref[b, :, pair_lanes[p]])
            for s in range(subs):
                sub = slice(s * SB_SUB, (s + 1) * SB_SUB)
                start = pl.multiple_of((i * TOK + (s + 1) * SB_SUB) - SB_KEYS, SB_SUB)
                lhs[b, p, s] = jnp.concatenate([q0[sub], q1[sub]], axis=0)
                ks[b, p, s] = k_ref[b, pl.ds(start, SB_KEYS), pair_lanes[p]]
                vs[b, p, s] = v_ref[b, pl.ds(start, SB_KEYS), pair_lanes[p]]
        ys = {t: _dot_nt(lhs[t], ks[t]) for t in tiles}
        log_betas, suffixes, sub_totals, sub_accs = {}, {}, {}, {}
        for t in tiles:
            log_betas[t], log_rest = _log2_sigmoid_pair(ys[t], BF16)
            log_rest = mask(log_rest)
            suffixes[t] = _dot(log_rest, tri)
            sub_totals[t] = suffixes[t][:, 0:1] + log_rest[:, 0:1].astype(F32)
        for t in tiles:
            w = mask(jnp.exp2(log_betas[t] + suffixes[t].astype(BF16)))
            sub_accs[t] = pick_head(_dot(w[:SB_SUB], vs[t]), _dot(w[SB_SUB:], vs[t]))
        accs = {(b, p): jnp.concatenate([sub_accs[b, p, s] for s in range(subs)], axis=0)
                for b, p in units}
        totals = {(b, p): jnp.concatenate(
            [sub_totals[b, p, s][hh * SB_SUB:(hh + 1) * SB_SUB]
             for hh in range(2) for s in range(subs)], axis=0) for b, p in units}
        return finish(accs, totals)

    @pl.when(i < SB_NEAR - 1)
    def _():
        near_first_blocks()

    @pl.when(i >= SB_NEAR - 1)
    def _():
        top = near()

        @pl.when(top >= F32_EXP2_UNDERFLOW)
        def _():
            tri = tri_ref[...]
            for b, p in units:
                for hh in range(2):
                    carry_ref[b, 2 * p + hh] = jnp.broadcast_to(
                        total_ref[b, p, hh * TOK:(hh + 1) * TOK, :], (TOK, TOK))
            row = lax.broadcasted_iota(jnp.int32, (TOK, TOK), 0)
            col = lax.broadcasted_iota(jnp.int32, (TOK, TOK), 1)
            uncovered = col < (row // SB_SUB + 1) * SB_SUB + (SB_NEAR - 1) * TOK - SB_KEYS

            def visit(j, partly_covered):
                blk = pl.ds(pl.multiple_of(j * TOK, TOK), TOK)
                for b, p in units:
                    qs = masked_heads(q_ref[b, :, pair_lanes[p]])
                    vs = masked_heads(v_ref[b, blk, pair_lanes[p]])
                    kp = k_ref[b, blk, pair_lanes[p]]
                    for hh in range(2):
                        h = 2 * p + hh
                        log_beta, log_rest = _log2_sigmoid_pair(_dot_nt(qs[hh], kp))
                        if partly_covered:
                            log_rest = jnp.where(uncovered, log_rest, 0.0)
                        sums = _dot(log_rest.astype(BF16), tri)
                        carry = carry_ref[b, h]
                        w = jnp.exp2(log_beta + sums[:, :TOK] + carry)
                        if partly_covered:
                            w = jnp.where(uncovered, w, 0.0)
                        acc_ref[b, p] += _dot(w.astype(BF16), vs[hh])
                        carry_ref[b, h] = carry + sums[:, TOK:]

            visit(i - (SB_NEAR - 1), True)

            def cond(state):
                d, far_top = state
                return jnp.logical_and(d <= i, far_top >= F32_EXP2_UNDERFLOW)

            def body(state):
                d, _ = state
                visit(i - d, False)
                return d + 1, jnp.max(carry_ref[...])

            lax.while_loop(cond, body, (jnp.int32(SB_NEAR), jnp.max(carry_ref[...])))
            write_out()


def _stick_breaking(proj, *, bb=4):
    bsz, seq, _ = proj.shape
    w = SB_HEADS * SB_DH
    first = proj.shape[2] // w - 4
    win = max(SB_KEYS, (SB_NEAR - 1) * TOK)
    j = np.arange(win)
    later = (j[:, None] > j[None, :]).astype(np.float32)
    tri = np.concatenate([later[:TOK, :TOK], np.ones((TOK, TOK), np.float32)], axis=1)
    blk = lambda c: pl.BlockSpec((bb, TOK, w), lambda b, i: (b, i, c))
    full = lambda c: pl.BlockSpec((bb, seq, w), lambda b, i: (b, 0, c))
    whole = lambda shape: pl.BlockSpec(shape, lambda b, i: (0,) * len(shape))
    return pl.pallas_call(
        functools.partial(_stick_breaking_kernel, bb=bb),
        grid=(bsz // bb, seq // TOK),
        in_specs=[blk(first), full(first + 1), full(first + 2), blk(first + 3),
                  whole((win, win)), whole((TOK, 2 * TOK))],
        out_specs=pl.BlockSpec((bb, TOK, w), lambda b, i: (b, i, 0)),
        out_shape=jax.ShapeDtypeStruct((bsz, seq, w), BF16),
        scratch_shapes=[pltpu.VMEM((bb, SB_HEADS // 2, TOK, LANES), F32),
                        pltpu.VMEM((bb, SB_HEADS // 2, 2 * TOK, 1), F32),
                        pltpu.VMEM((bb, SB_HEADS, TOK, TOK), F32)],
        compiler_params=pltpu.CompilerParams(dimension_semantics=("arbitrary", "arbitrary")),
        name="stick_breaking",
    )(proj, proj, proj, proj, jnp.asarray(later, dtype=BF16), jnp.asarray(tri, dtype=BF16))


HG_LEVELS = tuple(2 ** e for e in range(int(np.log2(TOK))))
HG_FIRST_VPU_LEVEL = SUBLANES // 2


def _half_boundary(cum3, last_rows, m):
    if 2 * m == SUBLANES:
        return jnp.broadcast_to(cum3[:, m - 1:m, :], cum3.shape)
    per_block = 2 * m // SUBLANES
    picks = [(g // per_block) * per_block + per_block // 2 - 1 for g in range(cum3.shape[0])]
    return jnp.concatenate([last_rows[g:g + 1] for g in picks], axis=0)


def _hgrn_consts():
    t = np.arange(TOK)
    tt, uu = t[:, None], t[None, :]
    mats = [(uu <= tt)]
    pair_masks = [np.eye(TOK, dtype=bool)]
    for m in HG_LEVELS:
        same_block = (tt // (2 * m)) == (uu // (2 * m))
        up_t = (tt % (2 * m)) >= m
        up_u = (uu % (2 * m)) >= m
        if m < HG_FIRST_VPU_LEVEL:
            mats.append(same_block & (up_t == up_u) & np.where(up_t, uu <= tt, uu > tt))
        pair_masks.append(same_block & up_t & ~up_u)
    prefix = np.concatenate(mats, axis=0).astype(np.float32)
    return jnp.asarray(prefix, dtype=BF16), jnp.asarray(np.stack(pair_masks), dtype=F32)


def _hgrn_kernel(q_ref, f_ref, i_ref, g_ref, lbl_ref, prefix_ref, pm_ref, o_ref, state_ref,
                 *, bb, layer):
    @pl.when(pl.program_id(1) == 0)
    def _():
        state_ref[...] = jnp.zeros_like(state_ref)

    logits = lbl_ref[...]
    e = jnp.exp(logits - jnp.max(logits, axis=0, keepdims=True))
    soft = e / jnp.sum(e, axis=0, keepdims=True)
    lb_all = jnp.zeros_like(soft[0:1])
    for r in range(1, layer + 1):
        lb_all = lb_all + soft[r:r + 1]
    x1 = jnp.log(lb_all)
    log_keep = jnp.log(1.0 - lb_all)
    prefix = prefix_ref[...]
    n_lv = len(HG_LEVELS)
    units = [(b, h) for b in range(bb) for h in range(HG_HEADS)]
    hl = lambda h: slice(h * LANES, (h + 1) * LANES)
    qs, kks, vs, states, cums, rests, level_sums = {}, {}, {}, {}, {}, {}, {}
    for b in range(bb):
        for p in range(HG_HEADS // 2):
            lanes = slice(2 * p * LANES, 2 * (p + 1) * LANES)
            fl = f_ref[b, :, lanes].astype(F32)
            ls = _log_sigmoid(fl)
            x2 = log_keep[:, lanes] + ls
            d = x1[:, lanes] - x2
            log_f = jnp.maximum(x1[:, lanes], x2) + jnp.log(1.0 + jnp.exp(jnp.minimum(d, -d)))
            pair_sums = _dot(prefix, (log_f * LOG2E).astype(BF16))
            cum = pair_sums[0:TOK]
            levels = [pair_sums[(1 + lv) * TOK:(2 + lv) * TOK]
                      for lv in range(n_lv) if HG_LEVELS[lv] < HG_FIRST_VPU_LEVEL]
            cum3 = cum.reshape(TOK // SUBLANES, SUBLANES, 2 * LANES)
            last_rows = jnp.broadcast_to(cum3[:, SUBLANES - 1:SUBLANES, :], cum3.shape)
            for m in HG_LEVELS[len(levels):]:
                gap = cum3 - _half_boundary(cum3, last_rows, m)
                levels.append(jnp.minimum(gap, -gap).reshape(TOK, 2 * LANES))
            rest = cum[TOK - 1:TOK, :] - cum
            kk = (1.0 - lb_all[:, lanes]) * jnp.exp(ls - fl)
            for hh in range(2):
                u = (b, 2 * p + hh)
                head = slice(hh * LANES, (hh + 1) * LANES)
                cums[u], rests[u], kks[u] = cum[:, head], rest[:, head], kk[:, head]
                level_sums[u] = [x[:, head] for x in levels]
    for b, h in units:
        qs[b, h] = q_ref[b, :, hl(h)].astype(F32)
        vs[b, h] = i_ref[b, :, hl(h)]
        states[b, h] = state_ref[b, h]
    attns = {}
    for u in units:
        q, kk = qs[u], kks[u]
        attn = _dot_nt(q.astype(BF16), kk.astype(BF16)) * pm_ref[0]
        for lv in range(n_lv):
            dec = jnp.exp2(level_sums[u][lv])
            attn = attn + _dot_nt((q * dec).astype(BF16), (kk * dec).astype(BF16)) * pm_ref[lv + 1]
        attns[u] = attn.astype(BF16)
    outs, kvs = {}, {}
    for u in units:
        ktail = (kks[u] * jnp.exp2(rests[u])).astype(BF16)
        q_in = (qs[u] * jnp.exp2(cums[u])).astype(BF16)
        outs[u] = _dot(attns[u], vs[u]) + _dot_nt(q_in, states[u].astype(BF16))
        kvs[u] = _dot_tn(vs[u], ktail)
    for b, h in units:
        o = outs[b, h]
        state_ref[b, h] = states[b, h] * jnp.exp2(cums[b, h][TOK - 1:TOK, :]) + kvs[b, h]
        ms = jnp.mean(o * o, axis=-1, keepdims=True)
        gate = _silu(g_ref[b, :, hl(h)].astype(F32))
        o_ref[b, :, hl(h)] = (o * lax.rsqrt(ms + EPS) * gate).astype(o_ref.dtype)


def _hgrn(proj, lb_logits, layer, *, bb=8):
    bsz, seq, _ = proj.shape
    w = HG_HEADS * LANES
    prefix, pair_masks = _hgrn_consts()
    blk = lambda c: pl.BlockSpec((bb, TOK, w), lambda b, i: (b, i, c))
    whole = lambda a: pl.BlockSpec(a.shape, lambda b, i: (0,) * a.ndim)
    return pl.pallas_call(
        functools.partial(_hgrn_kernel, bb=bb, layer=layer),
        grid=(bsz // bb, seq // TOK),
        in_specs=[blk(0), blk(1), blk(2), blk(3), whole(lb_logits), whole(prefix),
                  whole(pair_masks)],
        out_specs=pl.BlockSpec((bb, TOK, w), lambda b, i: (b, i, 0)),
        out_shape=jax.ShapeDtypeStruct((bsz, seq, w), BF16),
        scratch_shapes=[pltpu.VMEM((bb, HG_HEADS, LANES, LANES), F32)],
        compiler_params=pltpu.CompilerParams(dimension_semantics=("arbitrary", "arbitrary")),
        name="hgrn2",
    )(proj, proj, proj, proj, lb_logits, prefix, pair_masks)


LRU_TOK = 256
HALO = 8


def _lru_kernel(x_ref, g_ref, cw_ref, cb_ref, wa_ref, ba_ref, wx_ref, bx_ref, lam_ref, o_ref,
                xbuf_ref, h_ref, *, bb):
    c = pl.program_id(1)
    tok = x_ref.shape[1]

    @pl.when(c == 0)
    def _():
        xbuf_ref[...] = jnp.zeros_like(xbuf_ref)
        h_ref[...] = jnp.zeros_like(h_ref)

    lam = lam_ref[...]
    neg_sp = -(jnp.maximum(-lam, 0.0) + jnp.log(1.0 + jnp.exp(-jnp.abs(lam))))
    row = lax.broadcasted_iota(jnp.int32, (tok, LRU_WIDTH), 0)
    first_token = jnp.logical_and(row == 0, c == 0)
    sub_row = lax.broadcasted_iota(jnp.int32, (tok // SUBLANES, SUBLANES, LRU_WIDTH), 1)
    groups = LRU_WIDTH // LANES
    for b in range(bb):
        x = x_ref[b].astype(F32)
        xbuf_ref[b, HALO:HALO + tok, :] = x
        y = cb_ref[...] + x * cw_ref[CONV_W - 1:CONV_W, :]
        for j in range(CONV_W - 1):
            shift = CONV_W - 1 - j
            y = y + xbuf_ref[b, HALO - shift:HALO - shift + tok, :] * cw_ref[j:j + 1, :]
        xbuf_ref[b, 0:HALO, :] = x[tok - HALO:tok, :]
        ra, rx = [], []
        for gidx in range(groups):
            yg = y[:, gidx * LANES:(gidx + 1) * LANES].astype(BF16)
            ra.append(_dot(yg, wa_ref[gidx]))
            rx.append(_dot(yg, wx_ref[gidx]))
        r = jax.nn.sigmoid(jnp.concatenate(ra, axis=1) + ba_ref[...])
        ig = jax.nn.sigmoid(jnp.concatenate(rx, axis=1) + bx_ref[...])
        log_a = LRU_C * r * neg_sp
        a = jnp.exp(log_a)
        sq = -jnp.tanh(log_a) * (1.0 + a * a)
        mult = jnp.where(sq > 0.0, sq * lax.rsqrt(sq), 0.0)
        mult = jnp.where(first_token, 1.0, mult)
        u = mult * ig * y
        a = a.reshape(tok // SUBLANES, SUBLANES, LRU_WIDTH)
        u = u.reshape(tok // SUBLANES, SUBLANES, LRU_WIDTH)
        d = 1
        while d < SUBLANES:
            keep = sub_row >= d
            a_prev = jnp.where(keep, pltpu.roll(a, d, 1), 1.0)
            u_prev = jnp.where(keep, pltpu.roll(u, d, 1), 0.0)
            u = a * u_prev + u
            a = a * a_prev
            d *= 2
        h_in = h_ref[b]
        hs = []
        for j in range(tok // SUBLANES):
            hs.append(u[j] + a[j] * h_in)
            h_in = hs[-1][SUBLANES - 1:SUBLANES, :]
        h_ref[b] = h_in
        hcur = jnp.concatenate(hs, axis=0)
        o_ref[b] = (hcur * _silu(g_ref[b].astype(F32))).astype(o_ref.dtype)


def _block_diag_pairs(w):
    n, bw, _ = w.shape
    z = jnp.zeros((n // 2, bw, bw), w.dtype)
    top = jnp.concatenate([w[0::2], z], axis=2)
    bot = jnp.concatenate([z, w[1::2]], axis=2)
    return jnp.concatenate([top, bot], axis=1).astype(BF16)


def _lru(proj, conv_w, conv_b, w_a, b_a, w_x, b_x, lam, *, bb=4):
    bsz, seq, _ = proj.shape
    w = LRU_WIDTH
    first = proj.shape[2] // w - 2
    row = lambda a: a.reshape(1, w).astype(F32)
    blk = lambda c: pl.BlockSpec((bb, LRU_TOK, w), lambda b, i: (b, i, c))
    whole = lambda shape: pl.BlockSpec(shape, lambda b, i: (0,) * len(shape))
    groups = w // LANES
    return pl.pallas_call(
        functools.partial(_lru_kernel, bb=bb),
        grid=(bsz // bb, seq // LRU_TOK),
        in_specs=[blk(first), blk(first + 1), whole((CONV_W, w)), whole((1, w)),
                  whole((groups, LANES, LANES)), whole((1, w)),
                  whole((groups, LANES, LANES)), whole((1, w)), whole((1, w))],
        out_specs=pl.BlockSpec((bb, LRU_TOK, w), lambda b, i: (b, i, 0)),
        out_shape=jax.ShapeDtypeStruct((bsz, seq, w), BF16),
        scratch_shapes=[pltpu.VMEM((bb, HALO + LRU_TOK, w), F32), pltpu.VMEM((bb, 1, w), F32)],
        compiler_params=pltpu.CompilerParams(dimension_semantics=("arbitrary", "arbitrary")),
        name="rg_lru",
    )(proj, proj, conv_w.astype(F32), row(conv_b), _block_diag_pairs(w_a), row(b_a),
      _block_diag_pairs(w_x), row(b_x), row(lam))


def _even_in_weights(w_in):
    sq_first = 2 * RET_HEADS * RET_DK + 2 * RET_HEADS * RET_DV
    scale = np.ones((1, w_in.shape[1]), np.float32)
    scale[:, sq_first:sq_first + SB_HEADS * SB_DH] = SB_DH ** -0.5 * LOG2E
    return (w_in * scale).astype(BF16)


def kernel(x, pre_norm_w, post_norm_w, even_w_in, even_w_out, odd_w_in, odd_w_out, hgrn_lb_logits,
           conv_w, conv_b, lru_w_a, lru_b_a, lru_w_x, lru_b_x, lru_lambda):
    bsz, seq, d = x.shape
    depth = pre_norm_w.shape[0]
    x2d = x.reshape(bsz * seq, d)

    def in_weights(layer):
        if layer % 2 == 0:
            return _even_in_weights(even_w_in[layer // 2])
        return odd_w_in[layer // 2].astype(BF16)

    proj = _in_proj(x2d, pre_norm_w[0], in_weights(0))
    for layer in range(depth):
        idx = layer // 2
        proj = proj.reshape(bsz, seq, -1)
        if layer % 2 == 0:
            mix_a = _retention(proj)
            mix_b = _stick_breaking(proj)
            w_out = even_w_out[idx]
        else:
            mix_a = _hgrn(proj, hgrn_lb_logits.astype(F32), idx)
            mix_b = _lru(proj, conv_w[idx], conv_b[idx], lru_w_a[idx], lru_b_a[idx],
                         lru_w_x[idx], lru_b_x[idx], lru_lambda[idx])
            w_out = odd_w_out[idx]
        mix_a = mix_a.reshape(bsz * seq, -1)
        mix_b = mix_b.reshape(bsz * seq, -1)
        if layer + 1 < depth:
            x2d, proj = _out_in_proj(mix_a, mix_b, w_out.astype(BF16), post_norm_w[layer], x2d,
                                     pre_norm_w[layer + 1], in_weights(layer + 1))
        else:
            x2d = _out_proj(mix_a, mix_b, w_out.astype(BF16), post_norm_w[layer], x2d)
    return x2d.reshape(bsz, seq, d)
```

```python
import functools

import numpy as np
import jax
import jax.numpy as jnp
from jax import lax
from jax.experimental import pallas as pl
from jax.experimental.pallas import tpu as pltpu

F32 = jnp.float32
BF16 = jnp.bfloat16

EPS = 1e-6
LANES = 128
SUBLANES = 8
TOK = 128
ROPE_BASE = 10000.0
RET_HEADS, RET_DK, RET_DV = 4, 64, 128
SB_HEADS, SB_DH = 8, 64
HG_HEADS = 4
LRU_WIDTH, LRU_BLOCKS, CONV_W, LRU_C = 512, 8, 4, 8.0
LOG2E = float(np.log2(np.e))
LOG2E_HI = float(np.asarray(LOG2E, dtype=BF16))
LOG2E_LO = float(np.asarray(LOG2E - LOG2E_HI, dtype=BF16))
F32_EXP2_UNDERFLOW = -150.0


def _dot(a, b):
    return jnp.dot(a, b, preferred_element_type=F32)


def _dot_nt(a, b):
    return lax.dot_general(a, b, (((1,), (1,)), ((), ())), preferred_element_type=F32)


def _dot_tn(a, b):
    return lax.dot_general(a, b, (((0,), (0,)), ((), ())), preferred_element_type=F32)


def _log_sigmoid(z):
    return jnp.minimum(z, 0.0) - jnp.log(1.0 + jnp.exp(-jnp.abs(z)))


def _silu(g):
    return g * jax.nn.sigmoid(g)


def _in_proj_kernel(x_ref, nw_ref, w_ref, o_ref, *, n_chunk):
    x = x_ref[...]
    ms = jnp.mean(x * x, axis=-1, keepdims=True)
    h = (x * lax.rsqrt(ms + EPS) * nw_ref[...]).astype(BF16)
    for c in range(o_ref.shape[1] // n_chunk):
        cols = slice(c * n_chunk, (c + 1) * n_chunk)
        o_ref[:, cols] = _dot(h, w_ref[:, cols]).astype(o_ref.dtype)


def _in_proj(x2d, norm_w, w_bf16, *, tm=512, n_chunk=512):
    m, d = x2d.shape
    n = w_bf16.shape[1]
    return pl.pallas_call(
        functools.partial(_in_proj_kernel, n_chunk=n_chunk),
        grid=(m // tm,),
        in_specs=[
            pl.BlockSpec((tm, d), lambda i: (i, 0)),
            pl.BlockSpec((1, d), lambda i: (0, 0)),
            pl.BlockSpec((d, n), lambda i: (0, 0)),
        ],
        out_specs=pl.BlockSpec((tm, n), lambda i: (i, 0)),
        out_shape=jax.ShapeDtypeStruct((m, n), BF16),
        compiler_params=pltpu.CompilerParams(dimension_semantics=("arbitrary",)),
        name="in_proj",
    )(x2d, norm_w.reshape(1, d), w_bf16)


def _out_proj_kernel(ma_ref, mb_ref, w_ref, nw_ref, x_ref, o_ref):
    half = ma_ref.shape[1]
    y = _dot(ma_ref[...], w_ref[:half, :]) + _dot(mb_ref[...], w_ref[half:, :])
    ms = jnp.mean(y * y, axis=-1, keepdims=True)
    o_ref[...] = x_ref[...] + y * lax.rsqrt(ms + EPS) * nw_ref[...]


def _out_proj(mix_a, mix_b, w_bf16, norm_w, x2d, *, tm=1024):
    m, d = x2d.shape
    half = mix_a.shape[1]
    return pl.pallas_call(
        _out_proj_kernel,
        grid=(m // tm,),
        in_specs=[
            pl.BlockSpec((tm, half), lambda i: (i, 0)),
            pl.BlockSpec((tm, half), lambda i: (i, 0)),
            pl.BlockSpec((2 * half, d), lambda i: (0, 0)),
            pl.BlockSpec((1, d), lambda i: (0, 0)),
            pl.BlockSpec((tm, d), lambda i: (i, 0)),
        ],
        out_specs=pl.BlockSpec((tm, d), lambda i: (i, 0)),
        out_shape=jax.ShapeDtypeStruct((m, d), F32),
        compiler_params=pltpu.CompilerParams(dimension_semantics=("arbitrary",)),
        name="out_proj",
    )(mix_a, mix_b, w_bf16, norm_w.reshape(1, d), x2d)


def _out_in_proj_kernel(ma_ref, mb_ref, wo_ref, pw_ref, x_ref, nw_ref, wi_ref, xo_ref, proj_ref,
                        *, n_chunk, splits):
    half = ma_ref.shape[1]
    rows = x_ref.shape[0] // splits
    parts = [slice(s * rows, (s + 1) * rows) for s in range(splits)]
    ys = [_dot(ma_ref[r, :], wo_ref[:half, :]) + _dot(mb_ref[r, :], wo_ref[half:, :])
          for r in parts]
    hs = []
    for r, y in zip(parts, ys):
        ms = jnp.mean(y * y, axis=-1, keepdims=True)
        xn = x_ref[r, :] + y * lax.rsqrt(ms + EPS) * pw_ref[...]
        xo_ref[r, :] = xn
        ms = jnp.mean(xn * xn, axis=-1, keepdims=True)
        hs.append((xn * lax.rsqrt(ms + EPS) * nw_ref[...]).astype(BF16))
    for r, h in zip(parts, hs):
        for c in range(proj_ref.shape[1] // n_chunk):
            cols = slice(c * n_chunk, (c + 1) * n_chunk)
            proj_ref[r, cols] = _dot(h, wi_ref[:, cols]).astype(proj_ref.dtype)


def _out_in_proj(mix_a, mix_b, w_out_bf16, post_w, x2d, pre_w, w_in_bf16, *, tm=512, n_chunk=512,
                 splits=2):
    m, d = x2d.shape
    half = mix_a.shape[1]
    n = w_in_bf16.shape[1]
    tile = lambda width: pl.BlockSpec((tm, width), lambda i: (i, 0))
    whole = lambda shape: pl.BlockSpec(shape, lambda i: (0,) * len(shape))
    return pl.pallas_call(
        functools.partial(_out_in_proj_kernel, n_chunk=n_chunk, splits=splits),
        grid=(m // tm,),
        in_specs=[tile(half), tile(half), whole((2 * half, d)), whole((1, d)), tile(d),
                  whole((1, d)), whole((d, n))],
        out_specs=[tile(d), tile(n)],
        out_shape=[jax.ShapeDtypeStruct((m, d), F32), jax.ShapeDtypeStruct((m, n), BF16)],
        compiler_params=pltpu.CompilerParams(dimension_semantics=("arbitrary",)),
        name="out_in_proj",
    )(mix_a, mix_b, w_out_bf16, post_w.reshape(1, d), x2d, pre_w.reshape(1, d), w_in_bf16)


def _retention_kernel(q_ref, k_ref, v_ref, g_ref, rot_ref, cos_ref, sin_ref,
                      decay_ref, qd_ref, kd_ref, o_ref, state_ref, *, bb, g_chunk):
    @pl.when(pl.program_id(1) == 0)
    def _():
        state_ref[...] = jnp.zeros_like(state_ref)

    cos = cos_ref[...]
    sin = sin_ref[...]
    rot = rot_ref[...]
    q_rot = [_dot(q_ref[b], rot) for b in range(bb)]
    k_rot = [_dot(k_ref[b], rot) for b in range(bb)]
    lane = lax.broadcasted_iota(jnp.int32, (TOK, LANES), 1)
    head_mask = (lane < RET_DK, lane >= RET_DK)
    units = [(b, h) for b in range(bb) for h in range(RET_HEADS)]
    hl = lambda h: slice(h * LANES, (h + 1) * LANES)
    qm, kp, vs, states = {}, {}, {}, {}
    for b in range(bb):
        qr = q_ref[b].astype(F32) * cos + q_rot[b] * sin
        kr = (k_ref[b].astype(F32) * cos + k_rot[b] * sin) * (RET_DK ** -0.5)
        for h in range(RET_HEADS):
            lanes = hl(h // 2)
            qm[b, h] = jnp.where(head_mask[h % 2], qr[:, lanes], 0.0)
            kp[b, h] = kr[:, lanes].astype(BF16)
            vs[b, h] = v_ref[b, :, hl(h)]
            states[b, h] = state_ref[b, h]
    scores = {u: _dot_nt(qm[u].astype(BF16), kp[u]) for u in units}
    inter = {u: _dot((qm[u] * qd_ref[u[1]]).astype(BF16), states[u].astype(BF16)) for u in units}
    kv = {u: _dot_tn(kp[u], (vs[u].astype(F32) * kd_ref[u[1]]).astype(BF16)) for u in units}
    outs = {u: _dot((scores[u] * decay_ref[u[1]]).astype(BF16), vs[u]) + inter[u] for u in units}
    for b, h in units:
        o = outs[b, h]
        state_ref[b, h] = states[b, h] * g_chunk[h] + kv[b, h]
        ms = jnp.mean(o * o, axis=-1, keepdims=True)
        gate = _silu(g_ref[b, :, hl(h)].astype(F32))
        o_ref[b, :, hl(h)] = (o * lax.rsqrt(ms + EPS) * gate).astype(o_ref.dtype)


def _retention_consts(seq):
    h = np.arange(RET_HEADS, dtype=np.float64)
    log_g = np.log(1.0 - 2.0 ** (-5.0 - h))
    pos = np.arange(TOK, dtype=np.float64)
    dist = pos[:, None] - pos[None, :]
    decay = np.where(dist >= 0, np.exp(log_g[:, None, None] * np.maximum(dist, 0.0)), 0.0)
    q_decay = np.exp(log_g[:, None] * (pos + 1.0)[None, :])
    k_decay = np.exp(log_g[:, None] * (TOK - 1.0 - pos)[None, :])
    ones = np.ones((1, 1, LANES))
    g_chunk = tuple(float(v) for v in np.exp(log_g * TOK))
    half = RET_DK // 2
    inv = ROPE_BASE ** (-np.arange(half, dtype=np.float64) / half)
    ang = np.arange(seq, dtype=np.float64)[:, None] * inv[None, :]
    cos = np.tile(np.cos(ang), (1, 2 * RET_HEADS))
    sin = np.tile(np.sin(ang), (1, 2 * RET_HEADS))
    width = RET_HEADS * RET_DK
    col = np.arange(width)
    src = np.where(col % RET_DK < half, col + half, col - half)
    rot = np.zeros((width, width))
    rot[src, col] = np.where(col % RET_DK < half, -1.0, 1.0)
    to = lambda a: jnp.asarray(a, dtype=F32)
    return (jnp.asarray(rot, dtype=BF16), to(cos), to(sin), to(decay),
            to(q_decay[:, :, None] * ones), to(k_decay[:, :, None] * ones), g_chunk)


def _retention(proj, *, bb=8):
    bsz, seq, _ = proj.shape
    rot, cos, sin, decay, qd, kd, g_chunk = _retention_consts(seq)
    qk_w = RET_HEADS * RET_DK
    v_w = RET_HEADS * RET_DV
    qk_spec = lambda j: pl.BlockSpec((bb, TOK, qk_w), lambda b, c: (b, c, j))
    v_spec = lambda j: pl.BlockSpec((bb, TOK, v_w), lambda b, c: (b, c, j))
    tab_spec = pl.BlockSpec((TOK, qk_w), lambda b, c: (c, 0))
    const_spec = pl.BlockSpec((RET_HEADS, TOK, LANES), lambda b, c: (0, 0, 0))
    return pl.pallas_call(
        functools.partial(_retention_kernel, bb=bb, g_chunk=g_chunk),
        grid=(bsz // bb, seq // TOK),
        in_specs=[qk_spec(0), qk_spec(1), v_spec(1), v_spec(2),
                  pl.BlockSpec((qk_w, qk_w), lambda b, c: (0, 0)),
                  tab_spec, tab_spec, const_spec, const_spec, const_spec],
        out_specs=pl.BlockSpec((bb, TOK, v_w), lambda b, c: (b, c, 0)),
        out_shape=jax.ShapeDtypeStruct((bsz, seq, v_w), BF16),
        scratch_shapes=[pltpu.VMEM((bb, RET_HEADS, LANES, RET_DV), F32)],
        compiler_params=pltpu.CompilerParams(dimension_semantics=("arbitrary", "arbitrary")),
        name="retention",
    )(proj, proj, proj, proj, rot, cos, sin, decay, qd, kd)


SB_NEAR = 3
SB_SUB = 64
SB_KEYS = 256


def _log2_sigmoid_pair(y, dtype=F32):
    y = y.astype(dtype)
    soft = jnp.log(1.0 + jnp.exp2(-jnp.abs(y)))
    soft = soft * LOG2E if dtype == F32 else soft * LOG2E_HI + soft * LOG2E_LO
    log_beta = jnp.minimum(y, 0.0) - soft
    return log_beta, log_beta - y


def _stick_breaking_kernel(q_ref, k_ref, v_ref, g_ref, wtri_ref, tri_ref, o_ref,
                           acc_ref, total_ref, carry_ref, *, bb):
    i = pl.program_id(1)
    n_pairs = SB_HEADS // 2
    win = (SB_NEAR - 1) * TOK
    pair_lanes = [slice(p * LANES, (p + 1) * LANES) for p in range(n_pairs)]
    units = [(b, p) for b in range(bb) for p in range(n_pairs)]

    def pick_head(from_first, from_second):
        first = lax.broadcasted_iota(jnp.int32, from_first.shape, 1) < SB_DH
        return jnp.where(first, from_first, from_second)

    def masked_heads(x):
        first = lax.broadcasted_iota(jnp.int32, x.shape, 1) < SB_DH
        zero = jnp.zeros_like(x)
        return jnp.where(first, x, zero), jnp.where(first, zero, x)

    def write_out():
        for b, p in units:
            gate = _silu(g_ref[b, :, pair_lanes[p]].astype(F32))
            o_ref[b, :, pair_lanes[p]] = (acc_ref[b, p] * gate).astype(o_ref.dtype)

    def finish(accs, totals):
        gates = {(b, p): g_ref[b, :, pair_lanes[p]] for b, p in units}
        for b, p in units:
            acc_ref[b, p] = accs[b, p]
            total_ref[b, p] = totals[b, p]
            o_ref[b, :, pair_lanes[p]] = (
                accs[b, p] * _silu(gates[b, p].astype(F32))).astype(o_ref.dtype)
        return jnp.max(functools.reduce(jnp.maximum, totals.values()))


    def near_first_blocks():
        row = lax.broadcasted_iota(jnp.int32, (2 * TOK, win), 0) & (TOK - 1)
        col = lax.broadcasted_iota(jnp.int32, (2 * TOK, win), 1)
        strict = (col - row) < i * TOK
        qs = {(b, p): q_ref[b, :, pair_lanes[p]] for b, p in units}
        ks = {(b, p): k_ref[b, 0:win, pair_lanes[p]] for b, p in units}
        vs = {(b, p): v_ref[b, 0:win, pair_lanes[p]] for b, p in units}
        wtri = wtri_ref[0:win, 0:win]
        ys = {u: _dot_nt(jnp.concatenate(masked_heads(qs[u]), axis=0), ks[u]) for u in units}
        log_betas, suffixes, totals, accs = {}, {}, {}, {}
        for u in units:
            log_betas[u], log_rest = _log2_sigmoid_pair(ys[u], BF16)
            log_rest = jnp.where(strict, log_rest, 0.0)
            suffixes[u] = _dot(log_rest, wtri)
            totals[u] = suffixes[u][:, 0:1] + log_rest[:, 0:1].astype(F32)
        for u in units:
            w = jnp.where(strict, jnp.exp2(log_betas[u] + suffixes[u].astype(BF16)), 0.0)
            accs[u] = pick_head(_dot(w[:TOK], vs[u]), _dot(w[TOK:], vs[u]))
        return finish(accs, totals)

    def near():
        subs = TOK // SB_SUB
        rho = lax.broadcasted_iota(jnp.int32, (2 * SB_SUB, LANES), 0) & (SB_SUB - 1)
        col = lax.broadcasted_iota(jnp.int32, (2 * SB_SUB, LANES), 1)
        strict = col < rho + (LANES - SB_SUB)

        def mask(x):
            body = SB_KEYS - LANES
            return jnp.concatenate([x[:, :body], jnp.where(strict, x[:, body:], 0.0)], axis=1)

        tiles = [(b, p, s) for b, p in units for s in range(subs)]
        tri = wtri_ref[0:SB_KEYS, 0:SB_KEYS]
        lhs, ks, vs = {}, {}, {}
        for b, p in units:
            q0, q1 = masked_heads(q_ref[b, :, pair_lanes[p]])
            for s in range(subs):
                sub = slice(s * SB_SUB, (s + 1) * SB_SUB)
                start = pl.multiple_of((i * TOK + (s + 1) * SB_SUB) - SB_KEYS, SB_SUB)
                lhs[b, p, s] = jnp.concatenate([q0[sub], q1[sub]], axis=0)
                ks[b, p, s] = k_ref[b, pl.ds(start, SB_KEYS), pair_lanes[p]]
                vs[b, p, s] = v_ref[b, pl.ds(start, SB_KEYS), pair_lanes[p]]
        ys = {t: _dot_nt(lhs[t], ks[t]) for t in tiles}
        log_betas, suffixes, sub_totals, sub_accs = {}, {}, {}, {}
        for t in tiles:
            log_betas[t], log_rest = _log2_sigmoid_pair(ys[t], BF16)
            log_rest = mask(log_rest)
            suffixes[t] = _dot(log_rest, tri)
            sub_totals[t] = suffixes[t][:, 0:1] + log_rest[:, 0:1].astype(F32)
        for t in tiles:
            w = mask(jnp.exp2(log_betas[t] + suffixes[t].astype(BF16)))
            sub_accs[t] = pick_head(_dot(w[:SB_SUB], vs[t]), _dot(w[SB_SUB:], vs[t]))
        accs = {(b, p): jnp.concatenate([sub_accs[b, p, s] for s in range(subs)], axis=0)
                for b, p in units}
        totals = {(b, p): jnp.concatenate(
            [sub_totals[b, p, s][hh * SB_SUB:(hh + 1) * SB_SUB]
             for hh in range(2) for s in range(subs)], axis=0) for b, p in units}
        return finish(accs, totals)

    @pl.when(i < SB_NEAR - 1)
    def _():
        near_first_blocks()

    @pl.when(i >= SB_NEAR - 1)
    def _():
        top = near()

        @pl.when(top >= F32_EXP2_UNDERFLOW)
        def _():
            tri = tri_ref[...]
            for b, p in units:
                for hh in range(2):
                    carry_ref[b, 2 * p + hh] = jnp.broadcast_to(
                        total_ref[b, p, hh * TOK:(hh + 1) * TOK, :], (TOK, TOK))
            row = lax.broadcasted_iota(jnp.int32, (TOK, TOK), 0)
            col = lax.broadcasted_iota(jnp.int32, (TOK, TOK), 1)
            uncovered = col < (row // SB_SUB + 1) * SB_SUB + (SB_NEAR - 1) * TOK - SB_KEYS

            def visit(j, partly_covered):
                blk = pl.ds(pl.multiple_of(j * TOK, TOK), TOK)
                for b, p in units:
                    qs = masked_heads(q_ref[b, :, pair_lanes[p]])
                    vs = masked_heads(v_ref[b, blk, pair_lanes[p]])
                    kp = k_ref[b, blk, pair_lanes[p]]
                    for hh in range(2):
                        h = 2 * p + hh
                        log_beta, log_rest = _log2_sigmoid_pair(_dot_nt(qs[hh], kp))
                        if partly_covered:
                            log_rest = jnp.where(uncovered, log_rest, 0.0)
                        sums = _dot(log_rest.astype(BF16), tri)
                        carry = carry_ref[b, h]
                        w = jnp.exp2(log_beta + sums[:, :TOK] + carry)
                        if partly_covered:
                            w = jnp.where(uncovered, w, 0.0)
                        acc_ref[b, p] += _dot(w.astype(BF16), vs[hh])
                        carry_ref[b, h] = carry + sums[:, TOK:]

            visit(i - (SB_NEAR - 1), True)

            def cond(state):
                d, far_top = state
                return jnp.logical_and(d <= i, far_top >= F32_EXP2_UNDERFLOW)

            def body(state):
                d, _ = state
                visit(i - d, False)
                return d + 1, jnp.max(carry_ref[...])

            lax.while_loop(cond, body, (jnp.int32(SB_NEAR), jnp.max(carry_ref[...])))
            write_out()


def _stick_breaking(proj, *, bb=4):
    bsz, seq, _ = proj.shape
    w = SB_HEADS * SB_DH
    first = proj.shape[2] // w - 4
    win = max(SB_KEYS, (SB_NEAR - 1) * TOK)
    j = np.arange(win)
    later = (j[:, None] > j[None, :]).astype(np.float32)
    tri = np.concatenate([later[:TOK, :TOK], np.ones((TOK, TOK), np.float32)], axis=1)
    blk = lambda c: pl.BlockSpec((bb, TOK, w), lambda b, i: (b, i, c))
    full = lambda c: pl.BlockSpec((bb, seq, w), lambda b, i: (b, 0, c))
    whole = lambda shape: pl.BlockSpec(shape, lambda b, i: (0,) * len(shape))
    return pl.pallas_call(
        functools.partial(_stick_breaking_kernel, bb=bb),
        grid=(bsz // bb, seq // TOK),
        in_specs=[blk(first), full(first + 1), full(first + 2), blk(first + 3),
                  whole((win, win)), whole((TOK, 2 * TOK))],
        out_specs=pl.BlockSpec((bb, TOK, w), lambda b, i: (b, i, 0)),
        out_shape=jax.ShapeDtypeStruct((bsz, seq, w), BF16),
        scratch_shapes=[pltpu.VMEM((bb, SB_HEADS // 2, TOK, LANES), F32),
                        pltpu.VMEM((bb, SB_HEADS // 2, 2 * TOK, 1), F32),
                        pltpu.VMEM((bb, SB_HEADS, TOK, TOK), F32)],
        compiler_params=pltpu.CompilerParams(dimension_semantics=("arbitrary", "arbitrary")),
        name="stick_breaking",
    )(proj, proj, proj, proj, jnp.asarray(later, dtype=BF16), jnp.asarray(tri, dtype=BF16))


HG_LEVELS = tuple(2 ** e for e in range(int(np.log2(TOK))))
HG_FIRST_VPU_LEVEL = SUBLANES // 2


def _half_boundary(cum3, last_rows, m):
    if 2 * m == SUBLANES:
        return jnp.broadcast_to(cum3[:, m - 1:m, :], cum3.shape)
    per_block = 2 * m // SUBLANES
    picks = [(g // per_block) * per_block + per_block // 2 - 1 for g in range(cum3.shape[0])]
    return jnp.concatenate([last_rows[g:g + 1] for g in picks], axis=0)


def _hgrn_consts():
    t = np.arange(TOK)
    tt, uu = t[:, None], t[None, :]
    mats = [(uu <= tt)]
    pair_masks = [np.eye(TOK, dtype=bool)]
    for m in HG_LEVELS:
        same_block = (tt // (2 * m)) == (uu // (2 * m))
        up_t = (tt % (2 * m)) >= m
        up_u = (uu % (2 * m)) >= m
        if m < HG_FIRST_VPU_LEVEL:
            mats.append(same_block & (up_t == up_u) & np.where(up_t, uu <= tt, uu > tt))
        pair_masks.append(same_block & up_t & ~up_u)
    prefix = np.concatenate(mats, axis=0).astype(np.float32)
    return jnp.asarray(prefix, dtype=BF16), jnp.asarray(np.stack(pair_masks), dtype=F32)


def _hgrn_kernel(q_ref, f_ref, i_ref, g_ref, lbl_ref, prefix_ref, pm_ref, o_ref, state_ref,
                 *, bb, layer):
    @pl.when(pl.program_id(1) == 0)
    def _():
        state_ref[...] = jnp.zeros_like(state_ref)

    logits = lbl_ref[...]
    e = jnp.exp(logits - jnp.max(logits, axis=0, keepdims=True))
    soft = e / jnp.sum(e, axis=0, keepdims=True)
    lb_all = jnp.zeros_like(soft[0:1])
    for r in range(1, layer + 1):
        lb_all = lb_all + soft[r:r + 1]
    x1 = jnp.log(lb_all)
    log_keep = jnp.log(1.0 - lb_all)
    prefix = prefix_ref[...]
    n_lv = len(HG_LEVELS)
    units = [(b, h) for b in range(bb) for h in range(HG_HEADS)]
    hl = lambda h: slice(h * LANES, (h + 1) * LANES)
    qs, kks, vs, states, cums, rests, level_sums = {}, {}, {}, {}, {}, {}, {}
    for b in range(bb):
        for p in range(HG_HEADS // 2):
            lanes = slice(2 * p * LANES, 2 * (p + 1) * LANES)
            fl = f_ref[b, :, lanes].astype(F32)
            ls = _log_sigmoid(fl)
            x2 = log_keep[:, lanes] + ls
            d = x1[:, lanes] - x2
            log_f = jnp.maximum(x1[:, lanes], x2) + jnp.log(1.0 + jnp.exp(jnp.minimum(d, -d)))
            pair_sums = _dot(prefix, (log_f * LOG2E).astype(BF16))
            cum = pair_sums[0:TOK]
            levels = [pair_sums[(1 + lv) * TOK:(2 + lv) * TOK]
                      for lv in range(n_lv) if HG_LEVELS[lv] < HG_FIRST_VPU_LEVEL]
            cum3 = cum.reshape(TOK // SUBLANES, SUBLANES, 2 * LANES)
            last_rows = jnp.broadcast_to(cum3[:, SUBLANES - 1:SUBLANES, :], cum3.shape)
            for m in HG_LEVELS[len(levels):]:
                gap = cum3 - _half_boundary(cum3, last_rows, m)
                levels.append(jnp.minimum(gap, -gap).reshape(TOK, 2 * LANES))
            rest = cum[TOK - 1:TOK, :] - cum
            kk = (1.0 - lb_all[:, lanes]) * jnp.exp(ls - fl)
            for hh in range(2):
                u = (b, 2 * p + hh)
                head = slice(hh * LANES, (hh + 1) * LANES)
                cums[u], rests[u], kks[u] = cum[:, head], rest[:, head], kk[:, head]
                level_sums[u] = [x[:, head] for x in levels]
    for b, h in units:
        qs[b, h] = q_ref[b, :, hl(h)].astype(F32)
        vs[b, h] = i_ref[b, :, hl(h)]
        states[b, h] = state_ref[b, h]
    attns = {}
    for u in units:
        q, kk = qs[u], kks[u]
        attn = _dot_nt(q.astype(BF16), kk.astype(BF16)) * pm_ref[0]
        for lv in range(n_lv):
            dec = jnp.exp2(level_sums[u][lv])
            attn = attn + _dot_nt((q * dec).astype(BF16), (kk * dec).astype(BF16)) * pm_ref[lv + 1]
        attns[u] = attn.astype(BF16)
    outs, kvs = {}, {}
    for u in units:
        ktail = (kks[u] * jnp.exp2(rests[u])).astype(BF16)
        q_in = (qs[u] * jnp.exp2(cums[u])).astype(BF16)
        outs[u] = _dot(attns[u], vs[u]) + _dot_nt(q_in, states[u].astype(BF16))
        kvs[u] = _dot_tn(vs[u], ktail)
    for b, h in units:
        o = outs[b, h]
        state_ref[b, h] = states[b, h] * jnp.exp2(cums[b, h][TOK - 1:TOK, :]) + kvs[b, h]
        ms = jnp.mean(o * o, axis=-1, keepdims=True)
        gate = _silu(g_ref[b, :, hl(h)].astype(F32))
        o_ref[b, :, hl(h)] = (o * lax.rsqrt(ms + EPS) * gate).astype(o_ref.dtype)


def _hgrn(proj, lb_logits, layer, *, bb=8):
    bsz, seq, _ = proj.shape
    w = HG_HEADS * LANES
    prefix, pair_masks = _hgrn_consts()
    blk = lambda c: pl.BlockSpec((bb, TOK, w), lambda b, i: (b, i, c))
    whole = lambda a: pl.BlockSpec(a.shape, lambda b, i: (0,) * a.ndim)
    return pl.pallas_call(
        functools.partial(_hgrn_kernel, bb=bb, layer=layer),
        grid=(bsz // bb, seq // TOK),
        in_specs=[blk(0), blk(1), blk(2), blk(3), whole(lb_logits), whole(prefix),
                  whole(pair_masks)],
        out_specs=pl.BlockSpec((bb, TOK, w), lambda b, i: (b, i, 0)),
        out_shape=jax.ShapeDtypeStruct((bsz, seq, w), BF16),
        scratch_shapes=[pltpu.VMEM((bb, HG_HEADS, LANES, LANES), F32)],
        compiler_params=pltpu.CompilerParams(dimension_semantics=("arbitrary", "arbitrary")),
        name="hgrn2",
    )(proj, proj, proj, proj, lb_logits, prefix, pair_masks)


LRU_TOK = 256
HALO = 8


def _lru_kernel(x_ref, g_ref, cw_ref, cb_ref, wa_ref, ba_ref, wx_ref, bx_ref, lam_ref, o_ref,
                xbuf_ref, h_ref, *, bb):
    c = pl.program_id(1)
    tok = x_ref.shape[1]

    @pl.when(c == 0)
    def _():
        xbuf_ref[...] = jnp.zeros_like(xbuf_ref)
        h_ref[...] = jnp.zeros_like(h_ref)

    lam = lam_ref[...]
    neg_sp = -(jnp.maximum(-lam, 0.0) + jnp.log(1.0 + jnp.exp(-jnp.abs(lam))))
    row = lax.broadcasted_iota(jnp.int32, (tok, LRU_WIDTH), 0)
    first_token = jnp.logical_and(row == 0, c == 0)
    sub_row = lax.broadcasted_iota(jnp.int32, (tok // SUBLANES, SUBLANES, LRU_WIDTH), 1)
    groups = LRU_WIDTH // LANES
    for b in range(bb):
        x = x_ref[b].astype(F32)
        xbuf_ref[b, HALO:HALO + tok, :] = x
        y = cb_ref[...] + x * cw_ref[CONV_W - 1:CONV_W, :]
        for j in range(CONV_W - 1):
            shift = CONV_W - 1 - j
            y = y + xbuf_ref[b, HALO - shift:HALO - shift + tok, :] * cw_ref[j:j + 1, :]
        xbuf_ref[b, 0:HALO, :] = x[tok - HALO:tok, :]
        ra, rx = [], []
        for gidx in range(groups):
            yg = y[:, gidx * LANES:(gidx + 1) * LANES].astype(BF16)
            ra.append(_dot(yg, wa_ref[gidx]))
            rx.append(_dot(yg, wx_ref[gidx]))
        r = jax.nn.sigmoid(jnp.concatenate(ra, axis=1) + ba_ref[...])
        ig = jax.nn.sigmoid(jnp.concatenate(rx, axis=1) + bx_ref[...])
        log_a = LRU_C * r * neg_sp
        a = jnp.exp(log_a)
        sq = -jnp.tanh(log_a) * (1.0 + a * a)
        mult = jnp.where(sq > 0.0, sq * lax.rsqrt(sq), 0.0)
        mult = jnp.where(first_token, 1.0, mult)
        u = mult * ig * y
        a = a.reshape(tok // SUBLANES, SUBLANES, LRU_WIDTH)
        u = u.reshape(tok // SUBLANES, SUBLANES, LRU_WIDTH)
        d = 1
        while d < SUBLANES:
            keep = sub_row >= d
            a_prev = jnp.where(keep, pltpu.roll(a, d, 1), 1.0)
            u_prev = jnp.where(keep, pltpu.roll(u, d, 1), 0.0)
            u = a * u_prev + u
            a = a * a_prev
            d *= 2
        h_in = h_ref[b]
        hs = []
        for j in range(tok // SUBLANES):
            hs.append(u[j] + a[j] * h_in)
            h_in = hs[-1][SUBLANES - 1:SUBLANES, :]
        h_ref[b] = h_in
        hcur = jnp.concatenate(hs, axis=0)
        o_ref[b] = (hcur * _silu(g_ref[b].astype(F32))).astype(o_ref.dtype)


def _block_diag_pairs(w):
    n, bw, _ = w.shape
    z = jnp.zeros((n // 2, bw, bw), w.dtype)
    top = jnp.concatenate([w[0::2], z], axis=2)
    bot = jnp.concatenate([z, w[1::2]], axis=2)
    return jnp.concatenate([top, bot], axis=1).astype(BF16)


def _lru(proj, conv_w, conv_b, w_a, b_a, w_x, b_x, lam, *, bb=4):
    bsz, seq, _ = proj.shape
    w = LRU_WIDTH
    first = proj.shape[2] // w - 2
    row = lambda a: a.reshape(1, w).astype(F32)
    blk = lambda c: pl.BlockSpec((bb, LRU_TOK, w), lambda b, i: (b, i, c))
    whole = lambda shape: pl.BlockSpec(shape, lambda b, i: (0,) * len(shape))
    groups = w // LANES
    return pl.pallas_call(
        functools.partial(_lru_kernel, bb=bb),
        grid=(bsz // bb, seq // LRU_TOK),
        in_specs=[blk(first), blk(first + 1), whole((CONV_W, w)), whole((1, w)),
                  whole((groups, LANES, LANES)), whole((1, w)),
                  whole((groups, LANES, LANES)), whole((1, w)), whole((1, w))],
        out_specs=pl.BlockSpec((bb, LRU_TOK, w), lambda b, i: (b, i, 0)),
        out_shape=jax.ShapeDtypeStruct((bsz, seq, w), BF16),
        scratch_shapes=[pltpu.VMEM((bb, HALO + LRU_TOK, w), F32), pltpu.VMEM((bb, 1, w), F32)],
        compiler_params=pltpu.CompilerParams(dimension_semantics=("arbitrary", "arbitrary")),
        name="rg_lru",
    )(proj, proj, conv_w.astype(F32), row(conv_b), _block_diag_pairs(w_a), row(b_a),
      _block_diag_pairs(w_x), row(b_x), row(lam))


def _even_in_weights(w_in):
    sq_first = 2 * RET_HEADS * RET_DK + 2 * RET_HEADS * RET_DV
    scale = np.ones((1, w_in.shape[1]), np.float32)
    scale[:, sq_first:sq_first + SB_HEADS * SB_DH] = SB_DH ** -0.5 * LOG2E
    return (w_in * scale).astype(BF16)


def kernel(x, pre_norm_w, post_norm_w, even_w_in, even_w_out, odd_w_in, odd_w_out, hgrn_lb_logits,
           conv_w, conv_b, lru_w_a, lru_b_a, lru_w_x, lru_b_x, lru_lambda):
    bsz, seq, d = x.shape
    depth = pre_norm_w.shape[0]
    x2d = x.reshape(bsz * seq, d)

    def in_weights(layer):
        if layer % 2 == 0:
            return _even_in_weights(even_w_in[layer // 2])
        return odd_w_in[layer // 2].astype(BF16)

    proj = _in_proj(x2d, pre_norm_w[0], in_weights(0))
    for layer in range(depth):
        idx = layer // 2
        proj = proj.reshape(bsz, seq, -1)
        if layer % 2 == 0:
            mix_a = _retention(proj)
            mix_b = _stick_breaking(proj)
            w_out = even_w_out[idx]
        else:
            mix_a = _hgrn(proj, hgrn_lb_logits.astype(F32), idx)
            mix_b = _lru(proj, conv_w[idx], conv_b[idx], lru_w_a[idx], lru_b_a[idx],
                         lru_w_x[idx], lru_b_x[idx], lru_lambda[idx])
            w_out = odd_w_out[idx]
        mix_a = mix_a.reshape(bsz * seq, -1)
        mix_b = mix_b.reshape(bsz * seq, -1)
        if layer + 1 < depth:
            x2d, proj = _out_in_proj(mix_a, mix_b, w_out.astype(BF16), post_norm_w[layer], x2d,
                                     pre_norm_w[layer + 1], in_weights(layer + 1))
        else:
            x2d = _out_proj(mix_a, mix_b, w_out.astype(BF16), post_norm_w[layer], x2d)
    return x2d.reshape(bsz, seq, d)
```

```python
import functools

import numpy as np
import jax
import jax.numpy as jnp
from jax import lax
from jax.experimental import pallas as pl
from jax.experimental.pallas import tpu as pltpu

F32 = jnp.float32
BF16 = jnp.bfloat16

EPS = 1e-6
LANES = 128
SUBLANES = 8
TOK = 128
ROPE_BASE = 10000.0
RET_HEADS, RET_DK, RET_DV = 4, 64, 128
SB_HEADS, SB_DH = 8, 64
HG_HEADS = 4
LRU_WIDTH, LRU_BLOCKS, CONV_W, LRU_C = 512, 8, 4, 8.0
LOG2E = float(np.log2(np.e))
LOG2E_HI = float(np.asarray(LOG2E, dtype=BF16))
LOG2E_LO = float(np.asarray(LOG2E - LOG2E_HI, dtype=BF16))
F32_EXP2_UNDERFLOW = -150.0


def _dot(a, b):
    return jnp.dot(a, b, preferred_element_type=F32)


def _dot_nt(a, b):
    return lax.dot_general(a, b, (((1,), (1,)), ((), ())), preferred_element_type=F32)


def _dot_tn(a, b):
    return lax.dot_general(a, b, (((0,), (0,)), ((), ())), preferred_element_type=F32)


def _log_sigmoid(z):
    return jnp.minimum(z, 0.0) - jnp.log(1.0 + jnp.exp(-jnp.abs(z)))


def _silu(g):
    return g * jax.nn.sigmoid(g)


def _in_proj_kernel(x_ref, nw_ref, w_ref, o_ref, *, n_chunk):
    x = x_ref[...]
    ms = jnp.mean(x * x, axis=-1, keepdims=True)
    h = (x * lax.rsqrt(ms + EPS) * nw_ref[...]).astype(BF16)
    for c in range(o_ref.shape[1] // n_chunk):
        cols = slice(c * n_chunk, (c + 1) * n_chunk)
        o_ref[:, cols] = _dot(h, w_ref[:, cols]).astype(o_ref.dtype)


def _in_proj(x2d, norm_w, w_bf16, *, tm=1024, n_chunk=512):
    m, d = x2d.shape
    n = w_bf16.shape[1]
    return pl.pallas_call(
        functools.partial(_in_proj_kernel, n_chunk=n_chunk),
        grid=(m // tm,),
        in_specs=[
            pl.BlockSpec((tm, d), lambda i: (i, 0)),
            pl.BlockSpec((1, d), lambda i: (0, 0)),
            pl.BlockSpec((d, n), lambda i: (0, 0)),
        ],
        out_specs=pl.BlockSpec((tm, n), lambda i: (i, 0)),
        out_shape=jax.ShapeDtypeStruct((m, n), BF16),
        compiler_params=pltpu.CompilerParams(dimension_semantics=("arbitrary",)),
        name="in_proj",
    )(x2d, norm_w.reshape(1, d), w_bf16)


def _out_proj_kernel(ma_ref, mb_ref, w_ref, nw_ref, x_ref, o_ref):
    half = ma_ref.shape[1]
    y = _dot(ma_ref[...], w_ref[:half, :]) + _dot(mb_ref[...], w_ref[half:, :])
    ms = jnp.mean(y * y, axis=-1, keepdims=True)
    o_ref[...] = x_ref[...] + y * lax.rsqrt(ms + EPS) * nw_ref[...]


def _out_proj(mix_a, mix_b, w_bf16, norm_w, x2d, *, tm=1024):
    m, d = x2d.shape
    half = mix_a.shape[1]
    return pl.pallas_call(
        _out_proj_kernel,
        grid=(m // tm,),
        in_specs=[
            pl.BlockSpec((tm, half), lambda i: (i, 0)),
            pl.BlockSpec((tm, half), lambda i: (i, 0)),
            pl.BlockSpec((2 * half, d), lambda i: (0, 0)),
            pl.BlockSpec((1, d), lambda i: (0, 0)),
            pl.BlockSpec((tm, d), lambda i: (i, 0)),
        ],
        out_specs=pl.BlockSpec((tm, d), lambda i: (i, 0)),
        out_shape=jax.ShapeDtypeStruct((m, d), F32),
        compiler_params=pltpu.CompilerParams(dimension_semantics=("arbitrary",)),
        name="out_proj",
    )(mix_a, mix_b, w_bf16, norm_w.reshape(1, d), x2d)


def _out_in_proj_kernel(ma_ref, mb_ref, wo_ref, pw_ref, x_ref, nw_ref, wi_ref, xo_ref, proj_ref,
                        *, n_chunk, splits):
    half = ma_ref.shape[1]
    rows = x_ref.shape[0] // splits
    parts = [slice(s * rows, (s + 1) * rows) for s in range(splits)]
    ys = [_dot(ma_ref[r, :], wo_ref[:half, :]) + _dot(mb_ref[r, :], wo_ref[half:, :])
          for r in parts]
    hs = []
    for r, y in zip(parts, ys):
        ms = jnp.mean(y * y, axis=-1, keepdims=True)
        xn = x_ref[r, :] + y * lax.rsqrt(ms + EPS) * pw_ref[...]
        xo_ref[r, :] = xn
        ms = jnp.mean(xn * xn, axis=-1, keepdims=True)
        hs.append((xn * lax.rsqrt(ms + EPS) * nw_ref[...]).astype(BF16))
    for r, h in zip(parts, hs):
        for c in range(proj_ref.shape[1] // n_chunk):
            cols = slice(c * n_chunk, (c + 1) * n_chunk)
            proj_ref[r, cols] = _dot(h, wi_ref[:, cols]).astype(proj_ref.dtype)


def _out_in_proj(mix_a, mix_b, w_out_bf16, post_w, x2d, pre_w, w_in_bf16, *, tm=512, n_chunk=512,
                 splits=2):
    m, d = x2d.shape
    half = mix_a.shape[1]
    n = w_in_bf16.shape[1]
    tile = lambda width: pl.BlockSpec((tm, width), lambda i: (i, 0))
    whole = lambda shape: pl.BlockSpec(shape, lambda i: (0,) * len(shape))
    return pl.pallas_call(
        functools.partial(_out_in_proj_kernel, n_chunk=n_chunk, splits=splits),
        grid=(m // tm,),
        in_specs=[tile(half), tile(half), whole((2 * half, d)), whole((1, d)), tile(d),
                  whole((1, d)), whole((d, n))],
        out_specs=[tile(d), tile(n)],
        out_shape=[jax.ShapeDtypeStruct((m, d), F32), jax.ShapeDtypeStruct((m, n), BF16)],
        compiler_params=pltpu.CompilerParams(dimension_semantics=("arbitrary",)),
        name="out_in_proj",
    )(mix_a, mix_b, w_out_bf16, post_w.reshape(1, d), x2d, pre_w.reshape(1, d), w_in_bf16)


def _retention_kernel(q_ref, k_ref, v_ref, g_ref, rot_ref, cos_ref, sin_ref,
                      decay_ref, qd_ref, kd_ref, o_ref, state_ref, *, bb, g_chunk):
    @pl.when(pl.program_id(1) == 0)
    def _():
        state_ref[...] = jnp.zeros_like(state_ref)

    cos = cos_ref[...]
    sin = sin_ref[...]
    rot = rot_ref[...]
    q_rot = [_dot(q_ref[b], rot) for b in range(bb)]
    k_rot = [_dot(k_ref[b], rot) for b in range(bb)]
    lane = lax.broadcasted_iota(jnp.int32, (TOK, LANES), 1)
    head_mask = (lane < RET_DK, lane >= RET_DK)
    units = [(b, h) for b in range(bb) for h in range(RET_HEADS)]
    hl = lambda h: slice(h * LANES, (h + 1) * LANES)
    qm, kp, vs, states = {}, {}, {}, {}
    for b in range(bb):
        qr = q_ref[b].astype(F32) * cos + q_rot[b] * sin
        kr = (k_ref[b].astype(F32) * cos + k_rot[b] * sin) * (RET_DK ** -0.5)
        for h in range(RET_HEADS):
            lanes = hl(h // 2)
            qm[b, h] = jnp.where(head_mask[h % 2], qr[:, lanes], 0.0)
            kp[b, h] = kr[:, lanes].astype(BF16)
            vs[b, h] = v_ref[b, :, hl(h)]
            states[b, h] = state_ref[b, h]
    scores = {u: _dot_nt(qm[u].astype(BF16), kp[u]) for u in units}
    inter = {u: _dot((qm[u] * qd_ref[u[1]]).astype(BF16), states[u].astype(BF16)) for u in units}
    kv = {u: _dot_tn(kp[u], (vs[u].astype(F32) * kd_ref[u[1]]).astype(BF16)) for u in units}
    outs = {u: _dot((scores[u] * decay_ref[u[1]]).astype(BF16), vs[u]) + inter[u] for u in units}
    for b, h in units:
        o = outs[b, h]
        state_ref[b, h] = states[b, h] * g_chunk[h] + kv[b, h]
        ms = jnp.mean(o * o, axis=-1, keepdims=True)
        gate = _silu(g_ref[b, :, hl(h)].astype(F32))
        o_ref[b, :, hl(h)] = (o * lax.rsqrt(ms + EPS) * gate).astype(o_ref.dtype)


def _retention_consts(seq):
    h = np.arange(RET_HEADS, dtype=np.float64)
    log_g = np.log(1.0 - 2.0 ** (-5.0 - h))
    pos = np.arange(TOK, dtype=np.float64)
    dist = pos[:, None] - pos[None, :]
    decay = np.where(dist >= 0, np.exp(log_g[:, None, None] * np.maximum(dist, 0.0)), 0.0)
    q_decay = np.exp(log_g[:, None] * (pos + 1.0)[None, :])
    k_decay = np.exp(log_g[:, None] * (TOK - 1.0 - pos)[None, :])
    ones = np.ones((1, 1, LANES))
    g_chunk = tuple(float(v) for v in np.exp(log_g * TOK))
    half = RET_DK // 2
    inv = ROPE_BASE ** (-np.arange(half, dtype=np.float64) / half)
    ang = np.arange(seq, dtype=np.float64)[:, None] * inv[None, :]
    cos = np.tile(np.cos(ang), (1, 2 * RET_HEADS))
    sin = np.tile(np.sin(ang), (1, 2 * RET_HEADS))
    width = RET_HEADS * RET_DK
    col = np.arange(width)
    src = np.where(col % RET_DK < half, col + half, col - half)
    rot = np.zeros((width, width))
    rot[src, col] = np.where(col % RET_DK < half, -1.0, 1.0)
    to = lambda a: jnp.asarray(a, dtype=F32)
    return (jnp.asarray(rot, dtype=BF16), to(cos), to(sin), to(decay),
            to(q_decay[:, :, None] * ones), to(k_decay[:, :, None] * ones), g_chunk)


def _retention(proj, *, bb=16):
    bsz, seq, _ = proj.shape
    rot, cos, sin, decay, qd, kd, g_chunk = _retention_consts(seq)
    qk_w = RET_HEADS * RET_DK
    v_w = RET_HEADS * RET_DV
    qk_spec = lambda j: pl.BlockSpec((bb, TOK, qk_w), lambda b, c: (b, c, j))
    v_spec = lambda j: pl.BlockSpec((bb, TOK, v_w), lambda b, c: (b, c, j))
    tab_spec = pl.BlockSpec((TOK, qk_w), lambda b, c: (c, 0))
    const_spec = pl.BlockSpec((RET_HEADS, TOK, LANES), lambda b, c: (0, 0, 0))
    return pl.pallas_call(
        functools.partial(_retention_kernel, bb=bb, g_chunk=g_chunk),
        grid=(bsz // bb, seq // TOK),
        in_specs=[qk_spec(0), qk_spec(1), v_spec(1), v_spec(2),
                  pl.BlockSpec((qk_w, qk_w), lambda b, c: (0, 0)),
                  tab_spec, tab_spec, const_spec, const_spec, const_spec],
        out_specs=pl.BlockSpec((bb, TOK, v_w), lambda b, c: (b, c, 0)),
        out_shape=jax.ShapeDtypeStruct((bsz, seq, v_w), BF16),
        scratch_shapes=[pltpu.VMEM((bb, RET_HEADS, LANES, RET_DV), F32)],
        compiler_params=pltpu.CompilerParams(dimension_semantics=("arbitrary", "arbitrary")),
        name="retention",
    )(proj, proj, proj, proj, rot, cos, sin, decay, qd, kd)


SB_NEAR = 3
SB_SUB = 64
SB_KEYS = 256


def _log2_sigmoid_pair(y, dtype=F32):
    y = y.astype(dtype)
    soft = jnp.log(1.0 + jnp.exp2(-jnp.abs(y)))
    soft = soft * LOG2E if dtype == F32 else soft * LOG2E_HI + soft * LOG2E_LO
    log_beta = jnp.minimum(y, 0.0) - soft
    return log_beta, log_beta - y


def _stick_breaking_kernel(q_ref, k_ref, v_ref, g_ref, wtri_ref, tri_ref, o_ref,
                           acc_ref, total_ref, carry_ref, *, bb):
    i = pl.program_id(1)
    n_pairs = SB_HEADS // 2
    win = (SB_NEAR - 1) * TOK
    pair_lanes = [slice(p * LANES, (p + 1) * LANES) for p in range(n_pairs)]
    units = [(b, p) for b in range(bb) for p in range(n_pairs)]

    def pick_head(from_first, from_second):
        first = lax.broadcasted_iota(jnp.int32, from_first.shape, 1) < SB_DH
        return jnp.where(first, from_first, from_second)

    def masked_heads(x):
        first = lax.broadcasted_iota(jnp.int32, x.shape, 1) < SB_DH
        zero = jnp.zeros_like(x)
        return jnp.where(first, x, zero), jnp.where(first, zero, x)

    def write_out():
        for b, p in units:
            gate = _silu(g_ref[b, :, pair_lanes[p]].astype(F32))
            o_ref[b, :, pair_lanes[p]] = (acc_ref[b, p] * gate).astype(o_ref.dtype)

    def finish(accs, totals):
        gates = {(b, p): g_ref[b, :, pair_lanes[p]] for b, p in units}
        for b, p in units:
            acc_ref[b, p] = accs[b, p]
            total_ref[b, p] = totals[b, p]
            o_ref[b, :, pair_lanes[p]] = (
                accs[b, p] * _silu(gates[b, p].astype(F32))).astype(o_ref.dtype)
        return jnp.max(functools.reduce(jnp.maximum, totals.values()))


    def near_first_blocks():
        row = lax.broadcasted_iota(jnp.int32, (2 * TOK, win), 0) & (TOK - 1)
        col = lax.broadcasted_iota(jnp.int32, (2 * TOK, win), 1)
        strict = (col - row) < i * TOK
        qs = {(b, p): q_ref[b, :, pair_lanes[p]] for b, p in units}
        ks = {(b, p): k_ref[b, 0:win, pair_lanes[p]] for b, p in units}
        vs = {(b, p): v_ref[b, 0:win, pair_lanes[p]] for b, p in units}
        wtri = wtri_ref[0:win, 0:win]
        ys = {u: _dot_nt(jnp.concatenate(masked_heads(qs[u]), axis=0), ks[u]) for u in units}
        log_betas, suffixes, totals, accs = {}, {}, {}, {}
        for u in units:
            log_betas[u], log_rest = _log2_sigmoid_pair(ys[u], BF16)
            log_rest = jnp.where(strict, log_rest, 0.0)
            suffixes[u] = _dot(log_rest, wtri)
            totals[u] = suffixes[u][:, 0:1] + log_rest[:, 0:1].astype(F32)
        for u in units:
            w = jnp.where(strict, jnp.exp2(log_betas[u] + suffixes[u].astype(BF16)), 0.0)
            accs[u] = pick_head(_dot(w[:TOK], vs[u]), _dot(w[TOK:], vs[u]))
        return finish(accs, totals)

    def near():
        subs = TOK // SB_SUB
        rho = lax.broadcasted_iota(jnp.int32, (2 * SB_SUB, LANES), 0) & (SB_SUB - 1)
        col = lax.broadcasted_iota(jnp.int32, (2 * SB_SUB, LANES), 1)
        strict = col < rho + (LANES - SB_SUB)

        def mask(x):
            body = SB_KEYS - LANES
            return jnp.concatenate([x[:, :body], jnp.where(strict, x[:, body:], 0.0)], axis=1)

        tiles = [(b, p, s) for b, p in units for s in range(subs)]
        tri = wtri_ref[0:SB_KEYS, 0:SB_KEYS]
        lhs, ks, vs = {}, {}, {}
        for b, p in units:
            q0, q1 = masked_heads(q_ref[b, :, pair_lanes[p]])
            for s in range(subs):
                sub = slice(s * SB_SUB, (s + 1) * SB_SUB)
                start = pl.multiple_of((i * TOK + (s + 1) * SB_SUB) - SB_KEYS, SB_SUB)
                lhs[b, p, s] = jnp.concatenate([q0[sub], q1[sub]], axis=0)
                ks[b, p, s] = k_ref[b, pl.ds(start, SB_KEYS), pair_lanes[p]]
                vs[b, p, s] = v_ref[b, pl.ds(start, SB_KEYS), pair_lanes[p]]
        ys = {t: _dot_nt(lhs[t], ks[t]) for t in tiles}
        log_betas, suffixes, sub_totals, sub_accs = {}, {}, {}, {}
        for t in tiles:
            log_betas[t], log_rest = _log2_sigmoid_pair(ys[t], BF16)
            log_rest = mask(log_rest)
            suffixes[t] = _dot(log_rest, tri)
            sub_totals[t] = suffixes[t][:, 0:1] + log_rest[:, 0:1].astype(F32)
        for t in tiles:
            w = mask(jnp.exp2(log_betas[t] + suffixes[t].astype(BF16)))
            sub_accs[t] = pick_head(_dot(w[:SB_SUB], vs[t]), _dot(w[SB_SUB:], vs[t]))
        accs = {(b, p): jnp.concatenate([sub_accs[b, p, s] for s in range(subs)], axis=0)
                for b, p in units}
        totals = {(b, p): jnp.concatenate(
            [sub_totals[b, p, s][hh * SB_SUB:(hh + 1) * SB_SUB]
             for hh in range(2) for s in range(subs)], axis=0) for b, p in units}
        return finish(accs, totals)

    @pl.when(i < SB_NEAR - 1)
    def _():
        near_first_blocks()

    @pl.when(i >= SB_NEAR - 1)
    def _():
        top = near()

        @pl.when(top >= F32_EXP2_UNDERFLOW)
        def _():
            tri = tri_ref[...]
            for b, p in units:
                for hh in range(2):
                    carry_ref[b, 2 * p + hh] = jnp.broadcast_to(
                        total_ref[b, p, hh * TOK:(hh + 1) * TOK, :], (TOK, TOK))
            row = lax.broadcasted_iota(jnp.int32, (TOK, TOK), 0)
            col = lax.broadcasted_iota(jnp.int32, (TOK, TOK), 1)
            uncovered = col < (row // SB_SUB + 1) * SB_SUB + (SB_NEAR - 1) * TOK - SB_KEYS

            def visit(j, partly_covered):
                blk = pl.ds(pl.multiple_of(j * TOK, TOK), TOK)
                for b, p in units:
                    qs = masked_heads(q_ref[b, :, pair_lanes[p]])
                    vs = masked_heads(v_ref[b, blk, pair_lanes[p]])
                    kp = k_ref[b, blk, pair_lanes[p]]
                    for hh in range(2):
                        h = 2 * p + hh
                        log_beta, log_rest = _log2_sigmoid_pair(_dot_nt(qs[hh], kp))
                        if partly_covered:
                            log_rest = jnp.where(uncovered, log_rest, 0.0)
                        sums = _dot(log_rest.astype(BF16), tri)
                        carry = carry_ref[b, h]
                        w = jnp.exp2(log_beta + sums[:, :TOK] + carry)
                        if partly_covered:
                            w = jnp.where(uncovered, w, 0.0)
                        acc_ref[b, p] += _dot(w.astype(BF16), vs[hh])
                        carry_ref[b, h] = carry + sums[:, TOK:]

            visit(i - (SB_NEAR - 1), True)

            def cond(state):
                d, far_top = state
                return jnp.logical_and(d <= i, far_top >= F32_EXP2_UNDERFLOW)

            def body(state):
                d, _ = state
                visit(i - d, False)
                return d + 1, jnp.max(carry_ref[...])

            lax.while_loop(cond, body, (jnp.int32(SB_NEAR), jnp.max(carry_ref[...])))
            write_out()


def _stick_breaking(proj, *, bb=4):
    bsz, seq, _ = proj.shape
    w = SB_HEADS * SB_DH
    first = proj.shape[2] // w - 4
    win = max(SB_KEYS, (SB_NEAR - 1) * TOK)
    j = np.arange(win)
    later = (j[:, None] > j[None, :]).astype(np.float32)
    tri = np.concatenate([later[:TOK, :TOK], np.ones((TOK, TOK), np.float32)], axis=1)
    blk = lambda c: pl.BlockSpec((bb, TOK, w), lambda b, i: (b, i, c))
    full = lambda c: pl.BlockSpec((bb, seq, w), lambda b, i: (b, 0, c))
    whole = lambda shape: pl.BlockSpec(shape, lambda b, i: (0,) * len(shape))
    return pl.pallas_call(
        functools.partial(_stick_breaking_kernel, bb=bb),
        grid=(bsz // bb, seq // TOK),
        in_specs=[blk(first), full(first + 1), full(first + 2), blk(first + 3),
                  whole((win, win)), whole((TOK, 2 * TOK))],
        out_specs=pl.BlockSpec((bb, TOK, w), lambda b, i: (b, i, 0)),
        out_shape=jax.ShapeDtypeStruct((bsz, seq, w), BF16),
        scratch_shapes=[pltpu.VMEM((bb, SB_HEADS // 2, TOK, LANES), F32),
                        pltpu.VMEM((bb, SB_HEADS // 2, 2 * TOK, 1), F32),
                        pltpu.VMEM((bb, SB_HEADS, TOK, TOK), F32)],
        compiler_params=pltpu.CompilerParams(dimension_semantics=("arbitrary", "arbitrary")),
        name="stick_breaking",
    )(proj, proj, proj, proj, jnp.asarray(later, dtype=BF16), jnp.asarray(tri, dtype=BF16))


HG_LEVELS = tuple(2 ** e for e in range(int(np.log2(TOK))))
HG_FIRST_VPU_LEVEL = SUBLANES // 2


def _half_boundary(cum3, last_rows, m):
    if 2 * m == SUBLANES:
        return jnp.broadcast_to(cum3[:, m - 1:m, :], cum3.shape)
    per_block = 2 * m // SUBLANES
    picks = [(g // per_block) * per_block + per_block // 2 - 1 for g in range(cum3.shape[0])]
    return jnp.concatenate([last_rows[g:g + 1] for g in picks], axis=0)


def _hgrn_consts():
    t = np.arange(TOK)
    tt, uu = t[:, None], t[None, :]
    mats = [(uu <= tt)]
    pair_masks = [np.eye(TOK, dtype=bool)]
    for m in HG_LEVELS:
        same_block = (tt // (2 * m)) == (uu // (2 * m))
        up_t = (tt % (2 * m)) >= m
        up_u = (uu % (2 * m)) >= m
        if m < HG_FIRST_VPU_LEVEL:
            mats.append(same_block & (up_t == up_u) & np.where(up_t, uu <= tt, uu > tt))
        pair_masks.append(same_block & up_t & ~up_u)
    prefix = np.concatenate(mats, axis=0).astype(np.float32)
    return jnp.asarray(prefix, dtype=BF16), jnp.asarray(np.stack(pair_masks), dtype=F32)


def _hgrn_kernel(q_ref, f_ref, i_ref, g_ref, lbl_ref, prefix_ref, pm_ref, o_ref, state_ref,
                 *, bb, layer):
    @pl.when(pl.program_id(1) == 0)
    def _():
        state_ref[...] = jnp.zeros_like(state_ref)

    logits = lbl_ref[...]
    e = jnp.exp(logits - jnp.max(logits, axis=0, keepdims=True))
    soft = e / jnp.sum(e, axis=0, keepdims=True)
    lb_all = jnp.zeros_like(soft[0:1])
    for r in range(1, layer + 1):
        lb_all = lb_all + soft[r:r + 1]
    x1 = jnp.log(lb_all)
    log_keep = jnp.log(1.0 - lb_all)
    prefix = prefix_ref[...]
    n_lv = len(HG_LEVELS)
    units = [(b, h) for b in range(bb) for h in range(HG_HEADS)]
    hl = lambda h: slice(h * LANES, (h + 1) * LANES)
    qs, kks, vs, states, cums, rests, level_sums = {}, {}, {}, {}, {}, {}, {}
    for b in range(bb):
        for p in range(HG_HEADS // 2):
            lanes = slice(2 * p * LANES, 2 * (p + 1) * LANES)
            fl = f_ref[b, :, lanes].astype(F32)
            ls = _log_sigmoid(fl)
            x2 = log_keep[:, lanes] + ls
            d = x1[:, lanes] - x2
            log_f = jnp.maximum(x1[:, lanes], x2) + jnp.log(1.0 + jnp.exp(jnp.minimum(d, -d)))
            pair_sums = _dot(prefix, (log_f * LOG2E).astype(BF16))
            cum = pair_sums[0:TOK]
            levels = [pair_sums[(1 + lv) * TOK:(2 + lv) * TOK]
                      for lv in range(n_lv) if HG_LEVELS[lv] < HG_FIRST_VPU_LEVEL]
            cum3 = cum.reshape(TOK // SUBLANES, SUBLANES, 2 * LANES)
            last_rows = jnp.broadcast_to(cum3[:, SUBLANES - 1:SUBLANES, :], cum3.shape)
            for m in HG_LEVELS[len(levels):]:
                gap = cum3 - _half_boundary(cum3, last_rows, m)
                levels.append(jnp.minimum(gap, -gap).reshape(TOK, 2 * LANES))
            rest = cum[TOK - 1:TOK, :] - cum
            kk = (1.0 - lb_all[:, lanes]) * jnp.exp(ls - fl)
            for hh in range(2):
                u = (b, 2 * p + hh)
                head = slice(hh * LANES, (hh + 1) * LANES)
                cums[u], rests[u], kks[u] = cum[:, head], rest[:, head], kk[:, head]
                level_sums[u] = [x[:, head] for x in levels]
    for b, h in units:
        qs[b, h] = q_ref[b, :, hl(h)].astype(F32)
        vs[b, h] = i_ref[b, :, hl(h)]
        states[b, h] = state_ref[b, h]
    attns = {}
    for u in units:
        q, kk = qs[u], kks[u]
        attn = _dot_nt(q.astype(BF16), kk.astype(BF16)) * pm_ref[0]
        for lv in range(n_lv):
            dec = jnp.exp2(level_sums[u][lv])
            attn = attn + _dot_nt((q * dec).astype(BF16), (kk * dec).astype(BF16)) * pm_ref[lv + 1]
        attns[u] = attn.astype(BF16)
    outs, kvs = {}, {}
    for u in units:
        ktail = (kks[u] * jnp.exp2(rests[u])).astype(BF16)
        q_in = (qs[u] * jnp.exp2(cums[u])).astype(BF16)
        outs[u] = _dot(attns[u], vs[u]) + _dot_nt(q_in, states[u].astype(BF16))
        kvs[u] = _dot_tn(vs[u], ktail)
    for b, h in units:
        o = outs[b, h]
        state_ref[b, h] = states[b, h] * jnp.exp2(cums[b, h][TOK - 1:TOK, :]) + kvs[b, h]
        ms = jnp.mean(o * o, axis=-1, keepdims=True)
        gate = _silu(g_ref[b, :, hl(h)].astype(F32))
        o_ref[b, :, hl(h)] = (o * lax.rsqrt(ms + EPS) * gate).astype(o_ref.dtype)


def _hgrn(proj, lb_logits, layer, *, bb=8):
    bsz, seq, _ = proj.shape
    w = HG_HEADS * LANES
    prefix, pair_masks = _hgrn_consts()
    blk = lambda c: pl.BlockSpec((bb, TOK, w), lambda b, i: (b, i, c))
    whole = lambda a: pl.BlockSpec(a.shape, lambda b, i: (0,) * a.ndim)
    return pl.pallas_call(
        functools.partial(_hgrn_kernel, bb=bb, layer=layer),
        grid=(bsz // bb, seq // TOK),
        in_specs=[blk(0), blk(1), blk(2), blk(3), whole(lb_logits), whole(prefix),
                  whole(pair_masks)],
        out_specs=pl.BlockSpec((bb, TOK, w), lambda b, i: (b, i, 0)),
        out_shape=jax.ShapeDtypeStruct((bsz, seq, w), BF16),
        scratch_shapes=[pltpu.VMEM((bb, HG_HEADS, LANES, LANES), F32)],
        compiler_params=pltpu.CompilerParams(dimension_semantics=("arbitrary", "arbitrary")),
        name="hgrn2",
    )(proj, proj, proj, proj, lb_logits, prefix, pair_masks)


LRU_TOK = 256
HALO = 8


def _lru_kernel(x_ref, g_ref, cw_ref, cb_ref, wa_ref, ba_ref, wx_ref, bx_ref, lam_ref, o_ref,
                xbuf_ref, h_ref, *, bb):
    c = pl.program_id(1)
    tok = x_ref.shape[1]

    @pl.when(c == 0)
    def _():
        xbuf_ref[...] = jnp.zeros_like(xbuf_ref)
        h_ref[...] = jnp.zeros_like(h_ref)

    lam = lam_ref[...]
    neg_sp = -(jnp.maximum(-lam, 0.0) + jnp.log(1.0 + jnp.exp(-jnp.abs(lam))))
    row = lax.broadcasted_iota(jnp.int32, (tok, LRU_WIDTH), 0)
    first_token = jnp.logical_and(row == 0, c == 0)
    sub_row = lax.broadcasted_iota(jnp.int32, (tok // SUBLANES, SUBLANES, LRU_WIDTH), 1)
    groups = LRU_WIDTH // LANES
    for b in range(bb):
        x = x_ref[b].astype(F32)
        xbuf_ref[b, HALO:HALO + tok, :] = x
        y = cb_ref[...] + x * cw_ref[CONV_W - 1:CONV_W, :]
        for j in range(CONV_W - 1):
            shift = CONV_W - 1 - j
            y = y + xbuf_ref[b, HALO - shift:HALO - shift + tok, :] * cw_ref[j:j + 1, :]
        xbuf_ref[b, 0:HALO, :] = x[tok - HALO:tok, :]
        ra, rx = [], []
        for gidx in range(groups):
            yg = y[:, gidx * LANES:(gidx + 1) * LANES].astype(BF16)
            ra.append(_dot(yg, wa_ref[gidx]))
            rx.append(_dot(yg, wx_ref[gidx]))
        r = jax.nn.sigmoid(jnp.concatenate(ra, axis=1) + ba_ref[...])
        ig = jax.nn.sigmoid(jnp.concatenate(rx, axis=1) + bx_ref[...])
        log_a = LRU_C * r * neg_sp
        a = jnp.exp(log_a)
        sq = -jnp.tanh(log_a) * (1.0 + a * a)
        mult = jnp.where(sq > 0.0, sq * lax.rsqrt(sq), 0.0)
        mult = jnp.where(first_token, 1.0, mult)
        u = mult * ig * y
        a = a.reshape(tok // SUBLANES, SUBLANES, LRU_WIDTH)
        u = u.reshape(tok // SUBLANES, SUBLANES, LRU_WIDTH)
        d = 1
        while d < SUBLANES:
            keep = sub_row >= d
            a_prev = jnp.where(keep, pltpu.roll(a, d, 1), 1.0)
            u_prev = jnp.where(keep, pltpu.roll(u, d, 1), 0.0)
            u = a * u_prev + u
            a = a * a_prev
            d *= 2
        h_in = h_ref[b]
        hs = []
        for j in range(tok // SUBLANES):
            hs.append(u[j] + a[j] * h_in)
            h_in = hs[-1][SUBLANES - 1:SUBLANES, :]
        h_ref[b] = h_in
        hcur = jnp.concatenate(hs, axis=0)
        o_ref[b] = (hcur * _silu(g_ref[b].astype(F32))).astype(o_ref.dtype)


def _block_diag_pairs(w):
    n, bw, _ = w.shape
    z = jnp.zeros((n // 2, bw, bw), w.dtype)
    top = jnp.concatenate([w[0::2], z], axis=2)
    bot = jnp.concatenate([z, w[1::2]], axis=2)
    return jnp.concatenate([top, bot], axis=1).astype(BF16)


def _lru(proj, conv_w, conv_b, w_a, b_a, w_x, b_x, lam, *, bb=8):
    bsz, seq, _ = proj.shape
    w = LRU_WIDTH
    first = proj.shape[2] // w - 2
    row = lambda a: a.reshape(1, w).astype(F32)
    blk = lambda c: pl.BlockSpec((bb, LRU_TOK, w), lambda b, i: (b, i, c))
    whole = lambda shape: pl.BlockSpec(shape, lambda b, i: (0,) * len(shape))
    groups = w // LANES
    return pl.pallas_call(
        functools.partial(_lru_kernel, bb=bb),
        grid=(bsz // bb, seq // LRU_TOK),
        in_specs=[blk(first), blk(first + 1), whole((CONV_W, w)), whole((1, w)),
                  whole((groups, LANES, LANES)), whole((1, w)),
                  whole((groups, LANES, LANES)), whole((1, w)), whole((1, w))],
        out_specs=pl.BlockSpec((bb, LRU_TOK, w), lambda b, i: (b, i, 0)),
        out_shape=jax.ShapeDtypeStruct((bsz, seq, w), BF16),
        scratch_shapes=[pltpu.VMEM((bb, HALO + LRU_TOK, w), F32), pltpu.VMEM((bb, 1, w), F32)],
        compiler_params=pltpu.CompilerParams(dimension_semantics=("arbitrary", "arbitrary")),
        name="rg_lru",
    )(proj, proj, conv_w.astype(F32), row(conv_b), _block_diag_pairs(w_a), row(b_a),
      _block_diag_pairs(w_x), row(b_x), row(lam))


def _even_in_weights(w_in):
    sq_first = 2 * RET_HEADS * RET_DK + 2 * RET_HEADS * RET_DV
    scale = np.ones((1, w_in.shape[1]), np.float32)
    scale[:, sq_first:sq_first + SB_HEADS * SB_DH] = SB_DH ** -0.5 * LOG2E
    return (w_in * scale).astype(BF16)


def kernel(x, pre_norm_w, post_norm_w, even_w_in, even_w_out, odd_w_in, odd_w_out, hgrn_lb_logits,
           conv_w, conv_b, lru_w_a, lru_b_a, lru_w_x, lru_b_x, lru_lambda):
    bsz, seq, d = x.shape
    depth = pre_norm_w.shape[0]
    x2d = x.reshape(bsz * seq, d)

    def in_weights(layer):
        if layer % 2 == 0:
            return _even_in_weights(even_w_in[layer // 2])
        return odd_w_in[layer // 2].astype(BF16)

    proj = _in_proj(x2d, pre_norm_w[0], in_weights(0))
    for layer in range(depth):
        idx = layer // 2
        proj = proj.reshape(bsz, seq, -1)
        if layer % 2 == 0:
            mix_a = _retention(proj)
            mix_b = _stick_breaking(proj)
            w_out = even_w_out[idx]
        else:
            mix_a = _hgrn(proj, hgrn_lb_logits.astype(F32), idx)
            mix_b = _lru(proj, conv_w[idx], conv_b[idx], lru_w_a[idx], lru_b_a[idx],
                         lru_w_x[idx], lru_b_x[idx], lru_lambda[idx])
            w_out = odd_w_out[idx]
        mix_a = mix_a.reshape(bsz * seq, -1)
        mix_b = mix_b.reshape(bsz * seq, -1)
        if layer + 1 < depth:
            x2d, proj = _out_in_proj(mix_a, mix_b, w_out.astype(BF16), post_norm_w[layer], x2d,
                                     pre_norm_w[layer + 1], in_weights(layer + 1))
        else:
            x2d = _out_proj(mix_a, mix_b, w_out.astype(BF16), post_norm_w[layer], x2d)
    return x2d.reshape(bsz, seq, d)
```

```python
import functools

import numpy as np
import jax
import jax.numpy as jnp
from jax import lax
from jax.experimental import pallas as pl
from jax.experimental.pallas import tpu as pltpu

F32 = jnp.float32
BF16 = jnp.bfloat16

EPS = 1e-6
LANES = 128
SUBLANES = 8
TOK = 128
ROPE_BASE = 10000.0
RET_HEADS, RET_DK, RET_DV = 4, 64, 128
SB_HEADS, SB_DH = 8, 64
HG_HEADS = 4
LRU_WIDTH, LRU_BLOCKS, CONV_W, LRU_C = 512, 8, 4, 8.0
LOG2E = float(np.log2(np.e))
LOG2E_HI = float(np.asarray(LOG2E, dtype=BF16))
LOG2E_LO = float(np.asarray(LOG2E - LOG2E_HI, dtype=BF16))
F32_EXP2_UNDERFLOW = -150.0


def _dot(a, b):
    return jnp.dot(a, b, preferred_element_type=F32)


def _dot_nt(a, b):
    return lax.dot_general(a, b, (((1,), (1,)), ((), ())), preferred_element_type=F32)


def _dot_tn(a, b):
    return lax.dot_general(a, b, (((0,), (0,)), ((), ())), preferred_element_type=F32)


def _log_sigmoid(z):
    return jnp.minimum(z, 0.0) - jnp.log(1.0 + jnp.exp(-jnp.abs(z)))


def _silu(g):
    return g * jax.nn.sigmoid(g)


def _in_proj_kernel(x_ref, nw_ref, w_ref, o_ref, *, n_chunk):
    x = x_ref[...]
    ms = jnp.mean(x * x, axis=-1, keepdims=True)
    h = (x * lax.rsqrt(ms + EPS) * nw_ref[...]).astype(BF16)
    for c in range(o_ref.shape[1] // n_chunk):
        cols = slice(c * n_chunk, (c + 1) * n_chunk)
        o_ref[:, cols] = _dot(h, w_ref[:, cols]).astype(o_ref.dtype)


def _in_proj(x2d, norm_w, w_bf16, *, tm=1024, n_chunk=512):
    m, d = x2d.shape
    n = w_bf16.shape[1]
    return pl.pallas_call(
        functools.partial(_in_proj_kernel, n_chunk=n_chunk),
        grid=(m // tm,),
        in_specs=[
            pl.BlockSpec((tm, d), lambda i: (i, 0)),
            pl.BlockSpec((1, d), lambda i: (0, 0)),
            pl.BlockSpec((d, n), lambda i: (0, 0)),
        ],
        out_specs=pl.BlockSpec((tm, n), lambda i: (i, 0)),
        out_shape=jax.ShapeDtypeStruct((m, n), BF16),
        compiler_params=pltpu.CompilerParams(dimension_semantics=("arbitrary",)),
        name="in_proj",
    )(x2d, norm_w.reshape(1, d), w_bf16)


def _out_proj_kernel(ma_ref, mb_ref, w_ref, nw_ref, x_ref, o_ref):
    half = ma_ref.shape[1]
    y = _dot(ma_ref[...], w_ref[:half, :]) + _dot(mb_ref[...], w_ref[half:, :])
    ms = jnp.mean(y * y, axis=-1, keepdims=True)
    o_ref[...] = x_ref[...] + y * lax.rsqrt(ms + EPS) * nw_ref[...]


def _out_proj(mix_a, mix_b, w_bf16, norm_w, x2d, *, tm=1024):
    m, d = x2d.shape
    half = mix_a.shape[1]
    return pl.pallas_call(
        _out_proj_kernel,
        grid=(m // tm,),
        in_specs=[
            pl.BlockSpec((tm, half), lambda i: (i, 0)),
            pl.BlockSpec((tm, half), lambda i: (i, 0)),
            pl.BlockSpec((2 * half, d), lambda i: (0, 0)),
            pl.BlockSpec((1, d), lambda i: (0, 0)),
            pl.BlockSpec((tm, d), lambda i: (i, 0)),
        ],
        out_specs=pl.BlockSpec((tm, d), lambda i: (i, 0)),
        out_shape=jax.ShapeDtypeStruct((m, d), F32),
        compiler_params=pltpu.CompilerParams(dimension_semantics=("arbitrary",)),
        name="out_proj",
    )(mix_a, mix_b, w_bf16, norm_w.reshape(1, d), x2d)


def _out_in_proj_kernel(ma_ref, mb_ref, wo_ref, pw_ref, x_ref, nw_ref, wi_ref, xo_ref, proj_ref,
                        *, n_chunk, splits):
    half = ma_ref.shape[1]
    rows = x_ref.shape[0] // splits
    parts = [slice(s * rows, (s + 1) * rows) for s in range(splits)]
    ys = [_dot(ma_ref[r, :], wo_ref[:half, :]) + _dot(mb_ref[r, :], wo_ref[half:, :])
          for r in parts]
    hs = []
    for r, y in zip(parts, ys):
        ms = jnp.mean(y * y, axis=-1, keepdims=True)
        xn = x_ref[r, :] + y * lax.rsqrt(ms + EPS) * pw_ref[...]
        xo_ref[r, :] = xn
        ms = jnp.mean(xn * xn, axis=-1, keepdims=True)
        hs.append((xn * lax.rsqrt(ms + EPS) * nw_ref[...]).astype(BF16))
    for r, h in zip(parts, hs):
        for c in range(proj_ref.shape[1] // n_chunk):
            cols = slice(c * n_chunk, (c + 1) * n_chunk)
            proj_ref[r, cols] = _dot(h, wi_ref[:, cols]).astype(proj_ref.dtype)


def _out_in_proj(mix_a, mix_b, w_out_bf16, post_w, x2d, pre_w, w_in_bf16, *, tm=1024, n_chunk=512,
                 splits=2):
    m, d = x2d.shape
    half = mix_a.shape[1]
    n = w_in_bf16.shape[1]
    tile = lambda width: pl.BlockSpec((tm, width), lambda i: (i, 0))
    whole = lambda shape: pl.BlockSpec(shape, lambda i: (0,) * len(shape),
                                       pipeline_mode=pl.Buffered(1))
    return pl.pallas_call(
        functools.partial(_out_in_proj_kernel, n_chunk=n_chunk, splits=splits),
        grid=(m // tm,),
        in_specs=[tile(half), tile(half), whole((2 * half, d)), whole((1, d)), tile(d),
                  whole((1, d)), whole((d, n))],
        out_specs=[tile(d), tile(n)],
        out_shape=[jax.ShapeDtypeStruct((m, d), F32), jax.ShapeDtypeStruct((m, n), BF16)],
        compiler_params=pltpu.CompilerParams(dimension_semantics=("arbitrary",)),
        name="out_in_proj",
    )(mix_a, mix_b, w_out_bf16, post_w.reshape(1, d), x2d, pre_w.reshape(1, d), w_in_bf16)


def _retention_kernel(q_ref, k_ref, v_ref, g_ref, rot_ref, cos_ref, sin_ref,
                      decay_ref, qd_ref, kd_ref, o_ref, state_ref, *, bb, g_chunk):
    @pl.when(pl.program_id(1) == 0)
    def _():
        state_ref[...] = jnp.zeros_like(state_ref)

    cos = cos_ref[...]
    sin = sin_ref[...]
    rot = rot_ref[...]
    q_rot = [_dot(q_ref[b], rot) for b in range(bb)]
    k_rot = [_dot(k_ref[b], rot) for b in range(bb)]
    lane = lax.broadcasted_iota(jnp.int32, (TOK, LANES), 1)
    head_mask = (lane < RET_DK, lane >= RET_DK)
    units = [(b, h) for b in range(bb) for h in range(RET_HEADS)]
    hl = lambda h: slice(h * LANES, (h + 1) * LANES)
    qm, kp, vs, states = {}, {}, {}, {}
    for b in range(bb):
        qr = q_ref[b].astype(F32) * cos + q_rot[b] * sin
        kr = (k_ref[b].astype(F32) * cos + k_rot[b] * sin) * (RET_DK ** -0.5)
        for h in range(RET_HEADS):
            lanes = hl(h // 2)
            qm[b, h] = jnp.where(head_mask[h % 2], qr[:, lanes], 0.0)
            kp[b, h] = kr[:, lanes].astype(BF16)
            vs[b, h] = v_ref[b, :, hl(h)]
            states[b, h] = state_ref[b, h]
    scores = {u: _dot_nt(qm[u].astype(BF16), kp[u]) for u in units}
    inter = {u: _dot((qm[u] * qd_ref[u[1]]).astype(BF16), states[u].astype(BF16)) for u in units}
    kv = {u: _dot_tn(kp[u], (vs[u].astype(F32) * kd_ref[u[1]]).astype(BF16)) for u in units}
    outs = {u: _dot((scores[u] * decay_ref[u[1]]).astype(BF16), vs[u]) + inter[u] for u in units}
    for b, h in units:
        o = outs[b, h]
        state_ref[b, h] = states[b, h] * g_chunk[h] + kv[b, h]
        ms = jnp.mean(o * o, axis=-1, keepdims=True)
        gate = _silu(g_ref[b, :, hl(h)].astype(F32))
        o_ref[b, :, hl(h)] = (o * lax.rsqrt(ms + EPS) * gate).astype(o_ref.dtype)


def _retention_consts(seq):
    h = np.arange(RET_HEADS, dtype=np.float64)
    log_g = np.log(1.0 - 2.0 ** (-5.0 - h))
    pos = np.arange(TOK, dtype=np.float64)
    dist = pos[:, None] - pos[None, :]
    decay = np.where(dist >= 0, np.exp(log_g[:, None, None] * np.maximum(dist, 0.0)), 0.0)
    q_decay = np.exp(log_g[:, None] * (pos + 1.0)[None, :])
    k_decay = np.exp(log_g[:, None] * (TOK - 1.0 - pos)[None, :])
    ones = np.ones((1, 1, LANES))
    g_chunk = tuple(float(v) for v in np.exp(log_g * TOK))
    half = RET_DK // 2
    inv = ROPE_BASE ** (-np.arange(half, dtype=np.float64) / half)
    ang = np.arange(seq, dtype=np.float64)[:, None] * inv[None, :]
    cos = np.tile(np.cos(ang), (1, 2 * RET_HEADS))
    sin = np.tile(np.sin(ang), (1, 2 * RET_HEADS))
    width = RET_HEADS * RET_DK
    col = np.arange(width)
    src = np.where(col % RET_DK < half, col + half, col - half)
    rot = np.zeros((width, width))
    rot[src, col] = np.where(col % RET_DK < half, -1.0, 1.0)
    to = lambda a: jnp.asarray(a, dtype=F32)
    return (jnp.asarray(rot, dtype=BF16), to(cos), to(sin), to(decay),
            to(q_decay[:, :, None] * ones), to(k_decay[:, :, None] * ones), g_chunk)


def _retention(proj, *, bb=16):
    bsz, seq, _ = proj.shape
    rot, cos, sin, decay, qd, kd, g_chunk = _retention_consts(seq)
    qk_w = RET_HEADS * RET_DK
    v_w = RET_HEADS * RET_DV
    qk_spec = lambda j: pl.BlockSpec((bb, TOK, qk_w), lambda b, c: (b, c, j))
    v_spec = lambda j: pl.BlockSpec((bb, TOK, v_w), lambda b, c: (b, c, j))
    tab_spec = pl.BlockSpec((TOK, qk_w), lambda b, c: (c, 0))
    const_spec = pl.BlockSpec((RET_HEADS, TOK, LANES), lambda b, c: (0, 0, 0))
    return pl.pallas_call(
        functools.partial(_retention_kernel, bb=bb, g_chunk=g_chunk),
        grid=(bsz // bb, seq // TOK),
        in_specs=[qk_spec(0), qk_spec(1), v_spec(1), v_spec(2),
                  pl.BlockSpec((qk_w, qk_w), lambda b, c: (0, 0)),
                  tab_spec, tab_spec, const_spec, const_spec, const_spec],
        out_specs=pl.BlockSpec((bb, TOK, v_w), lambda b, c: (b, c, 0)),
        out_shape=jax.ShapeDtypeStruct((bsz, seq, v_w), BF16),
        scratch_shapes=[pltpu.VMEM((bb, RET_HEADS, LANES, RET_DV), F32)],
        compiler_params=pltpu.CompilerParams(dimension_semantics=("arbitrary", "arbitrary")),
        name="retention",
    )(proj, proj, proj, proj, rot, cos, sin, decay, qd, kd)


SB_NEAR = 3
SB_SUB = 64
SB_KEYS = 256


def _log2_sigmoid_pair(y, dtype=F32):
    y = y.astype(dtype)
    soft = jnp.log(1.0 + jnp.exp2(-jnp.abs(y)))
    soft = soft * LOG2E if dtype == F32 else soft * LOG2E_HI + soft * LOG2E_LO
    log_beta = jnp.minimum(y, 0.0) - soft
    return log_beta, log_beta - y


def _stick_breaking_kernel(q_ref, k_ref, v_ref, g_ref, wtri_ref, tri_ref, o_ref,
                           acc_ref, total_ref, carry_ref, *, bb):
    i = pl.program_id(1)
    n_pairs = SB_HEADS // 2
    win = (SB_NEAR - 1) * TOK
    pair_lanes = [slice(p * LANES, (p + 1) * LANES) for p in range(n_pairs)]
    units = [(b, p) for b in range(bb) for p in range(n_pairs)]

    def pick_head(from_first, from_second):
        first = lax.broadcasted_iota(jnp.int32, from_first.shape, 1) < SB_DH
        return jnp.where(first, from_first, from_second)

    def masked_heads(x):
        first = lax.broadcasted_iota(jnp.int32, x.shape, 1) < SB_DH
        zero = jnp.zeros_like(x)
        return jnp.where(first, x, zero), jnp.where(first, zero, x)

    def write_out():
        for b, p in units:
            gate = _silu(g_ref[b, :, pair_lanes[p]].astype(F32))
            o_ref[b, :, pair_lanes[p]] = (acc_ref[b, p] * gate).astype(o_ref.dtype)

    def finish(accs, totals):
        gates = {(b, p): g_ref[b, :, pair_lanes[p]] for b, p in units}
        for b, p in units:
            acc_ref[b, p] = accs[b, p]
            total_ref[b, p] = totals[b, p]
            o_ref[b, :, pair_lanes[p]] = (
                accs[b, p] * _silu(gates[b, p].astype(F32))).astype(o_ref.dtype)
        return jnp.max(functools.reduce(jnp.maximum, totals.values()))


    def near_first_blocks():
        row = lax.broadcasted_iota(jnp.int32, (2 * TOK, win), 0) & (TOK - 1)
        col = lax.broadcasted_iota(jnp.int32, (2 * TOK, win), 1)
        strict = (col - row) < i * TOK
        qs = {(b, p): q_ref[b, :, pair_lanes[p]] for b, p in units}
        ks = {(b, p): k_ref[b, 0:win, pair_lanes[p]] for b, p in units}
        vs = {(b, p): v_ref[b, 0:win, pair_lanes[p]] for b, p in units}
        wtri = wtri_ref[0:win, 0:win]
        ys = {u: _dot_nt(jnp.concatenate(masked_heads(qs[u]), axis=0), ks[u]) for u in units}
        log_betas, suffixes, totals, accs = {}, {}, {}, {}
        for u in units:
            log_betas[u], log_rest = _log2_sigmoid_pair(ys[u], BF16)
            log_rest = jnp.where(strict, log_rest, 0.0)
            suffixes[u] = _dot(log_rest, wtri)
            totals[u] = suffixes[u][:, 0:1] + log_rest[:, 0:1].astype(F32)
        for u in units:
            w = jnp.where(strict, jnp.exp2(log_betas[u] + suffixes[u].astype(BF16)), 0.0)
            accs[u] = pick_head(_dot(w[:TOK], vs[u]), _dot(w[TOK:], vs[u]))
        return finish(accs, totals)

    def near():
        subs = TOK // SB_SUB
        rho = lax.broadcasted_iota(jnp.int32, (2 * SB_SUB, LANES), 0) & (SB_SUB - 1)
        col = lax.broadcasted_iota(jnp.int32, (2 * SB_SUB, LANES), 1)
        strict = col < rho + (LANES - SB_SUB)

        def mask(x):
            body = SB_KEYS - LANES
            return jnp.concatenate([x[:, :body], jnp.where(strict, x[:, body:], 0.0)], axis=1)

        tiles = [(b, p, s) for b, p in units for s in range(subs)]
        tri = wtri_ref[0:SB_KEYS, 0:SB_KEYS]
        lhs, ks, vs = {}, {}, {}
        for b, p in units:
            q0, q1 = masked_heads(q_ref[b, :, pair_lanes[p]])
            for s in range(subs):
                sub = slice(s * SB_SUB, (s + 1) * SB_SUB)
                start = pl.multiple_of((i * TOK + (s + 1) * SB_SUB) - SB_KEYS, SB_SUB)
                lhs[b, p, s] = jnp.concatenate([q0[sub], q1[sub]], axis=0)
                ks[b, p, s] = k_ref[b, pl.ds(start, SB_KEYS), pair_lanes[p]]
                vs[b, p, s] = v_ref[b, pl.ds(start, SB_KEYS), pair_lanes[p]]
        ys = {t: _dot_nt(lhs[t], ks[t]) for t in tiles}
        log_betas, suffixes, sub_totals, sub_accs = {}, {}, {}, {}
        for t in tiles:
            log_betas[t], log_rest = _log2_sigmoid_pair(ys[t], BF16)
            log_rest = mask(log_rest)
            suffixes[t] = _dot(log_rest, tri)
            sub_totals[t] = suffixes[t][:, 0:1] + log_rest[:, 0:1].astype(F32)
        for t in tiles:
            w = mask(jnp.exp2(log_betas[t] + suffixes[t].astype(BF16)))
            sub_accs[t] = pick_head(_dot(w[:SB_SUB], vs[t]), _dot(w[SB_SUB:], vs[t]))
        accs = {(b, p): jnp.concatenate([sub_accs[b, p, s] for s in range(subs)], axis=0)
                for b, p in units}
        totals = {(b, p): jnp.concatenate(
            [sub_totals[b, p, s][hh * SB_SUB:(hh + 1) * SB_SUB]
             for hh in range(2) for s in range(subs)], axis=0) for b, p in units}
        return finish(accs, totals)

    @pl.when(i < SB_NEAR - 1)
    def _():
        near_first_blocks()

    @pl.when(i >= SB_NEAR - 1)
    def _():
        top = near()

        @pl.when(top >= F32_EXP2_UNDERFLOW)
        def _():
            tri = tri_ref[...]
            for b, p in units:
                for hh in range(2):
                    carry_ref[b, 2 * p + hh] = jnp.broadcast_to(
                        total_ref[b, p, hh * TOK:(hh + 1) * TOK, :], (TOK, TOK))
            row = lax.broadcasted_iota(jnp.int32, (TOK, TOK), 0)
            col = lax.broadcasted_iota(jnp.int32, (TOK, TOK), 1)
            uncovered = col < (row // SB_SUB + 1) * SB_SUB + (SB_NEAR - 1) * TOK - SB_KEYS

            def visit(j, partly_covered):
                blk = pl.ds(pl.multiple_of(j * TOK, TOK), TOK)
                for b, p in units:
                    qs = masked_heads(q_ref[b, :, pair_lanes[p]])
                    vs = masked_heads(v_ref[b, blk, pair_lanes[p]])
                    kp = k_ref[b, blk, pair_lanes[p]]
                    for hh in range(2):
                        h = 2 * p + hh
                        log_beta, log_rest = _log2_sigmoid_pair(_dot_nt(qs[hh], kp))
                        if partly_covered:
                            log_rest = jnp.where(uncovered, log_rest, 0.0)
                        sums = _dot(log_rest.astype(BF16), tri)
                        carry = carry_ref[b, h]
                        w = jnp.exp2(log_beta + sums[:, :TOK] + carry)
                        if partly_covered:
                            w = jnp.where(uncovered, w, 0.0)
                        acc_ref[b, p] += _dot(w.astype(BF16), vs[hh])
                        carry_ref[b, h] = carry + sums[:, TOK:]

            visit(i - (SB_NEAR - 1), True)

            def cond(state):
                d, far_top = state
                return jnp.logical_and(d <= i, far_top >= F32_EXP2_UNDERFLOW)

            def body(state):
                d, _ = state
                visit(i - d, False)
                return d + 1, jnp.max(carry_ref[...])

            lax.while_loop(cond, body, (jnp.int32(SB_NEAR), jnp.max(carry_ref[...])))
            write_out()


def _stick_breaking(proj, *, bb=4):
    bsz, seq, _ = proj.shape
    w = SB_HEADS * SB_DH
    first = proj.shape[2] // w - 4
    win = max(SB_KEYS, (SB_NEAR - 1) * TOK)
    j = np.arange(win)
    later = (j[:, None] > j[None, :]).astype(np.float32)
    tri = np.concatenate([later[:TOK, :TOK], np.ones((TOK, TOK), np.float32)], axis=1)
    blk = lambda c: pl.BlockSpec((bb, TOK, w), lambda b, i: (b, i, c))
    full = lambda c: pl.BlockSpec((bb, seq, w), lambda b, i: (b, 0, c))
    whole = lambda shape: pl.BlockSpec(shape, lambda b, i: (0,) * len(shape))
    return pl.pallas_call(
        functools.partial(_stick_breaking_kernel, bb=bb),
        grid=(bsz // bb, seq // TOK),
        in_specs=[blk(first), full(first + 1), full(first + 2), blk(first + 3),
                  whole((win, win)), whole((TOK, 2 * TOK))],
        out_specs=pl.BlockSpec((bb, TOK, w), lambda b, i: (b, i, 0)),
        out_shape=jax.ShapeDtypeStruct((bsz, seq, w), BF16),
        scratch_shapes=[pltpu.VMEM((bb, SB_HEADS // 2, TOK, LANES), F32),
                        pltpu.VMEM((bb, SB_HEADS // 2, 2 * TOK, 1), F32),
                        pltpu.VMEM((bb, SB_HEADS, TOK, TOK), F32)],
        compiler_params=pltpu.CompilerParams(dimension_semantics=("arbitrary", "arbitrary")),
        name="stick_breaking",
    )(proj, proj, proj, proj, jnp.asarray(later, dtype=BF16), jnp.asarray(tri, dtype=BF16))


HG_LEVELS = tuple(2 ** e for e in range(int(np.log2(TOK))))
HG_FIRST_VPU_LEVEL = SUBLANES // 2


def _half_boundary(cum3, last_rows, m):
    if 2 * m == SUBLANES:
        return jnp.broadcast_to(cum3[:, m - 1:m, :], cum3.shape)
    per_block = 2 * m // SUBLANES
    picks = [(g // per_block) * per_block + per_block // 2 - 1 for g in range(cum3.shape[0])]
    return jnp.concatenate([last_rows[g:g + 1] for g in picks], axis=0)


def _hgrn_consts():
    t = np.arange(TOK)
    tt, uu = t[:, None], t[None, :]
    mats = [(uu <= tt)]
    pair_masks = [np.eye(TOK, dtype=bool)]
    for m in HG_LEVELS:
        same_block = (tt // (2 * m)) == (uu // (2 * m))
        up_t = (tt % (2 * m)) >= m
        up_u = (uu % (2 * m)) >= m
        if m < HG_FIRST_VPU_LEVEL:
            mats.append(same_block & (up_t == up_u) & np.where(up_t, uu <= tt, uu > tt))
        pair_masks.append(same_block & up_t & ~up_u)
    prefix = np.concatenate(mats, axis=0).astype(np.float32)
    return jnp.asarray(prefix, dtype=BF16), jnp.asarray(np.stack(pair_masks), dtype=F32)


def _hgrn_kernel(q_ref, f_ref, i_ref, g_ref, lbl_ref, prefix_ref, pm_ref, o_ref, state_ref,
                 *, bb, layer):
    @pl.when(pl.program_id(1) == 0)
    def _():
        state_ref[...] = jnp.zeros_like(state_ref)

    logits = lbl_ref[...]
    e = jnp.exp(logits - jnp.max(logits, axis=0, keepdims=True))
    soft = e / jnp.sum(e, axis=0, keepdims=True)
    lb_all = jnp.zeros_like(soft[0:1])
    for r in range(1, layer + 1):
        lb_all = lb_all + soft[r:r + 1]
    x1 = jnp.log(lb_all)
    log_keep = jnp.log(1.0 - lb_all)
    prefix = prefix_ref[...]
    n_lv = len(HG_LEVELS)
    units = [(b, h) for b in range(bb) for h in range(HG_HEADS)]
    hl = lambda h: slice(h * LANES, (h + 1) * LANES)
    qs, kks, vs, states, cums, rests, level_sums = {}, {}, {}, {}, {}, {}, {}
    for b in range(bb):
        for p in range(HG_HEADS // 2):
            lanes = slice(2 * p * LANES, 2 * (p + 1) * LANES)
            fl = f_ref[b, :, lanes].astype(F32)
            ls = _log_sigmoid(fl)
            x2 = log_keep[:, lanes] + ls
            d = x1[:, lanes] - x2
            log_f = jnp.maximum(x1[:, lanes], x2) + jnp.log(1.0 + jnp.exp(jnp.minimum(d, -d)))
            pair_sums = _dot(prefix, (log_f * LOG2E).astype(BF16))
            cum = pair_sums[0:TOK]
            levels = [pair_sums[(1 + lv) * TOK:(2 + lv) * TOK]
                      for lv in range(n_lv) if HG_LEVELS[lv] < HG_FIRST_VPU_LEVEL]
            cum3 = cum.reshape(TOK // SUBLANES, SUBLANES, 2 * LANES)
            last_rows = jnp.broadcast_to(cum3[:, SUBLANES - 1:SUBLANES, :], cum3.shape)
            for m in HG_LEVELS[len(levels):]:
                gap = cum3 - _half_boundary(cum3, last_rows, m)
                levels.append(jnp.minimum(gap, -gap).reshape(TOK, 2 * LANES))
            rest = cum[TOK - 1:TOK, :] - cum
            kk = (1.0 - lb_all[:, lanes]) * jnp.exp(ls - fl)
            for hh in range(2):
                u = (b, 2 * p + hh)
                head = slice(hh * LANES, (hh + 1) * LANES)
                cums[u], rests[u], kks[u] = cum[:, head], rest[:, head], kk[:, head]
                level_sums[u] = [x[:, head] for x in levels]
    for b, h in units:
        qs[b, h] = q_ref[b, :, hl(h)].astype(F32)
        vs[b, h] = i_ref[b, :, hl(h)]
        states[b, h] = state_ref[b, h]
    attns = {}
    for u in units:
        q, kk = qs[u], kks[u]
        attn = _dot_nt(q.astype(BF16), kk.astype(BF16)) * pm_ref[0]
        for lv in range(n_lv):
            dec = jnp.exp2(level_sums[u][lv])
            attn = attn + _dot_nt((q * dec).astype(BF16), (kk * dec).astype(BF16)) * pm_ref[lv + 1]
        attns[u] = attn.astype(BF16)
    outs, kvs = {}, {}
    for u in units:
        ktail = (kks[u] * jnp.exp2(rests[u])).astype(BF16)
        q_in = (qs[u] * jnp.exp2(cums[u])).astype(BF16)
        outs[u] = _dot(attns[u], vs[u]) + _dot_nt(q_in, states[u].astype(BF16))
        kvs[u] = _dot_tn(vs[u], ktail)
    for b, h in units:
        o = outs[b, h]
        state_ref[b, h] = states[b, h] * jnp.exp2(cums[b, h][TOK - 1:TOK, :]) + kvs[b, h]
        ms = jnp.mean(o * o, axis=-1, keepdims=True)
        gate = _silu(g_ref[b, :, hl(h)].astype(F32))
        o_ref[b, :, hl(h)] = (o * lax.rsqrt(ms + EPS) * gate).astype(o_ref.dtype)


def _hgrn(proj, lb_logits, layer, *, bb=8):
    bsz, seq, _ = proj.shape
    w = HG_HEADS * LANES
    prefix, pair_masks = _hgrn_consts()
    blk = lambda c: pl.BlockSpec((bb, TOK, w), lambda b, i: (b, i, c))
    whole = lambda a: pl.BlockSpec(a.shape, lambda b, i: (0,) * a.ndim)
    return pl.pallas_call(
        functools.partial(_hgrn_kernel, bb=bb, layer=layer),
        grid=(bsz // bb, seq // TOK),
        in_specs=[blk(0), blk(1), blk(2), blk(3), whole(lb_logits), whole(prefix),
                  whole(pair_masks)],
        out_specs=pl.BlockSpec((bb, TOK, w), lambda b, i: (b, i, 0)),
        out_shape=jax.ShapeDtypeStruct((bsz, seq, w), BF16),
        scratch_shapes=[pltpu.VMEM((bb, HG_HEADS, LANES, LANES), F32)],
        compiler_params=pltpu.CompilerParams(dimension_semantics=("arbitrary", "arbitrary")),
        name="hgrn2",
    )(proj, proj, proj, proj, lb_logits, prefix, pair_masks)


LRU_TOK = 256
HALO = 8


def _lru_kernel(x_ref, g_ref, cw_ref, cb_ref, wa_ref, ba_ref, wx_ref, bx_ref, lam_ref, o_ref,
                xbuf_ref, h_ref, *, bb):
    c = pl.program_id(1)
    tok = x_ref.shape[1]

    @pl.when(c == 0)
    def _():
        xbuf_ref[...] = jnp.zeros_like(xbuf_ref)
        h_ref[...] = jnp.zeros_like(h_ref)

    lam = lam_ref[...]
    neg_sp = -(jnp.maximum(-lam, 0.0) + jnp.log(1.0 + jnp.exp(-jnp.abs(lam))))
    row = lax.broadcasted_iota(jnp.int32, (tok, LRU_WIDTH), 0)
    first_token = jnp.logical_and(row == 0, c == 0)
    sub_row = lax.broadcasted_iota(jnp.int32, (tok // SUBLANES, SUBLANES, LRU_WIDTH), 1)
    groups = LRU_WIDTH // LANES
    for b in range(bb):
        x = x_ref[b].astype(F32)
        xbuf_ref[b, HALO:HALO + tok, :] = x
        y = cb_ref[...] + x * cw_ref[CONV_W - 1:CONV_W, :]
        for j in range(CONV_W - 1):
            shift = CONV_W - 1 - j
            y = y + xbuf_ref[b, HALO - shift:HALO - shift + tok, :] * cw_ref[j:j + 1, :]
        xbuf_ref[b, 0:HALO, :] = x[tok - HALO:tok, :]
        ra, rx = [], []
        for gidx in range(groups):
            yg = y[:, gidx * LANES:(gidx + 1) * LANES].astype(BF16)
            ra.append(_dot(yg, wa_ref[gidx]))
            rx.append(_dot(yg, wx_ref[gidx]))
        r = jax.nn.sigmoid(jnp.concatenate(ra, axis=1) + ba_ref[...])
        ig = jax.nn.sigmoid(jnp.concatenate(rx, axis=1) + bx_ref[...])
        log_a = LRU_C * r * neg_sp
        a = jnp.exp(log_a)
        sq = -jnp.tanh(log_a) * (1.0 + a * a)
        mult = jnp.where(sq > 0.0, sq * lax.rsqrt(sq), 0.0)
        mult = jnp.where(first_token, 1.0, mult)
        u = mult * ig * y
        a = a.reshape(tok // SUBLANES, SUBLANES, LRU_WIDTH)
        u = u.reshape(tok // SUBLANES, SUBLANES, LRU_WIDTH)
        d = 1
        while d < SUBLANES:
            keep = sub_row >= d
            a_prev = jnp.where(keep, pltpu.roll(a, d, 1), 1.0)
            u_prev = jnp.where(keep, pltpu.roll(u, d, 1), 0.0)
            u = a * u_prev + u
            a = a * a_prev
            d *= 2
        h_in = h_ref[b]
        hs = []
        for j in range(tok // SUBLANES):
            hs.append(u[j] + a[j] * h_in)
            h_in = hs[-1][SUBLANES - 1:SUBLANES, :]
        h_ref[b] = h_in
        hcur = jnp.concatenate(hs, axis=0)
        o_ref[b] = (hcur * _silu(g_ref[b].astype(F32))).astype(o_ref.dtype)


def _block_diag_pairs(w):
    n, bw, _ = w.shape
    z = jnp.zeros((n // 2, bw, bw), w.dtype)
    top = jnp.concatenate([w[0::2], z], axis=2)
    bot = jnp.concatenate([z, w[1::2]], axis=2)
    return jnp.concatenate([top, bot], axis=1).astype(BF16)


def _lru(proj, conv_w, conv_b, w_a, b_a, w_x, b_x, lam, *, bb=8):
    bsz, seq, _ = proj.shape
    w = LRU_WIDTH
    first = proj.shape[2] // w - 2
    row = lambda a: a.reshape(1, w).astype(F32)
    blk = lambda c: pl.BlockSpec((bb, LRU_TOK, w), lambda b, i: (b, i, c))
    whole = lambda shape: pl.BlockSpec(shape, lambda b, i: (0,) * len(shape))
    groups = w // LANES
    return pl.pallas_call(
        functools.partial(_lru_kernel, bb=bb),
        grid=(bsz // bb, seq // LRU_TOK),
        in_specs=[blk(first), blk(first + 1), whole((CONV_W, w)), whole((1, w)),
                  whole((groups, LANES, LANES)), whole((1, w)),
                  whole((groups, LANES, LANES)), whole((1, w)), whole((1, w))],
        out_specs=pl.BlockSpec((bb, LRU_TOK, w), lambda b, i: (b, i, 0)),
        out_shape=jax.ShapeDtypeStruct((bsz, seq, w), BF16),
        scratch_shapes=[pltpu.VMEM((bb, HALO + LRU_TOK, w), F32), pltpu.VMEM((bb, 1, w), F32)],
        compiler_params=pltpu.CompilerParams(dimension_semantics=("arbitrary", "arbitrary")),
        name="rg_lru",
    )(proj, proj, conv_w.astype(F32), row(conv_b), _block_diag_pairs(w_a), row(b_a),
      _block_diag_pairs(w_x), row(b_x), row(lam))


def _even_in_weights(w_in):
    sq_first = 2 * RET_HEADS * RET_DK + 2 * RET_HEADS * RET_DV
    scale = np.ones((1, w_in.shape[1]), np.float32)
    scale[:, sq_first:sq_first + SB_HEADS * SB_DH] = SB_DH ** -0.5 * LOG2E
    return (w_in * scale).astype(BF16)


def kernel(x, pre_norm_w, post_norm_w, even_w_in, even_w_out, odd_w_in, odd_w_out, hgrn_lb_logits,
           conv_w, conv_b, lru_w_a, lru_b_a, lru_w_x, lru_b_x, lru_lambda):
    bsz, seq, d = x.shape
    depth = pre_norm_w.shape[0]
    x2d = x.reshape(bsz * seq, d)

    def in_weights(layer):
        if layer % 2 == 0:
            return _even_in_weights(even_w_in[layer // 2])
        return odd_w_in[layer // 2].astype(BF16)

    proj = _in_proj(x2d, pre_norm_w[0], in_weights(0))
    for layer in range(depth):
        idx = layer // 2
        proj = proj.reshape(bsz, seq, -1)
        if layer % 2 == 0:
            mix_a = _retention(proj)
            mix_b = _stick_breaking(proj)
            w_out = even_w_out[idx]
        else:
            mix_a = _hgrn(proj, hgrn_lb_logits.astype(F32), idx)
            mix_b = _lru(proj, conv_w[idx], conv_b[idx], lru_w_a[idx], lru_b_a[idx],
                         lru_w_x[idx], lru_b_x[idx], lru_lambda[idx])
            w_out = odd_w_out[idx]
        mix_a = mix_a.reshape(bsz * seq, -1)
        mix_b = mix_b.reshape(bsz * seq, -1)
        if layer + 1 < depth:
            x2d, proj = _out_in_proj(mix_a, mix_b, w_out.astype(BF16), post_norm_w[layer], x2d,
                                     pre_norm_w[layer + 1], in_weights(layer + 1))
        else:
            x2d = _out_proj(mix_a, mix_b, w_out.astype(BF16), post_norm_w[layer], x2d)
    return x2d.reshape(bsz, seq, d)
```
